```python
import jax, jax.numpy as jnp
from jax import lax
import numpy as np

D_MODEL = 1024
BATCH = 2
SEQ = 8192
DEPTH = 1
DEC_BATCH = 128
DEC_SEQ = 1
PAST_LEN = 16384
PAGE_SIZE = 128

H_MLA = 8
NOPE_DIM = 64
ROPE_DIM = 32
V_DIM = 64
Q_LORA = 256
KV_LORA = 128
H_FOX = 8
HD_FOX = 64
D_MIX = H_MLA * V_DIM + H_FOX * HD_FOX
D_FF = 2816
CONV_W = 3
QBLK = 128
ROPE_THETA = 10000.0
EPS = 1e-6
NEG_INF = -1e30
MLA_SCALE = (NOPE_DIM + ROPE_DIM) ** -0.5
FOX_SCALE = HD_FOX ** -0.5

OFF_QA = 0
OFF_KVA = OFF_QA + Q_LORA
OFF_KR = OFF_KVA + KV_LORA
OFF_FQ = OFF_KR + ROPE_DIM
OFF_FK = OFF_FQ + H_FOX * HD_FOX
OFF_FV = OFF_FK + H_FOX * HD_FOX
OFF_FF = OFF_FV + H_FOX * HD_FOX
IN_COLS = OFF_FF + H_FOX

kernel_name = "hymba_mla_fox_convffn_step"


def rms_norm(x, g):
    xf = x.astype(jnp.float32)
    y = xf * lax.rsqrt(jnp.mean(xf * xf, axis=-1, keepdims=True) + EPS)
    return (y * g.astype(jnp.float32)).astype(x.dtype)


def apply_rope(x, pos):
    half = x.shape[-1] // 2
    inv = ROPE_THETA ** (-jnp.arange(half, dtype=jnp.float32) / half)
    ang = pos.astype(jnp.float32)[:, None] * inv[None, :]
    ang = ang.reshape((1, ang.shape[0]) + (1,) * (x.ndim - 3) + (half,))
    cos, sin = jnp.cos(ang), jnp.sin(ang)
    xf = x.astype(jnp.float32)
    x1, x2 = xf[..., :half], xf[..., half:]
    return jnp.concatenate([x1 * cos - x2 * sin, x1 * sin + x2 * cos], axis=-1).astype(x.dtype)


def mixer_inputs(h, pos, p):
    B, T, _ = h.shape
    z = h @ p["w_in"]
    q_a = rms_norm(z[..., OFF_QA:OFF_KVA], p["g_qa"])
    c_kv = rms_norm(z[..., OFF_KVA:OFF_KR], p["g_kva"])
    k_rope = apply_rope(rms_norm(z[..., OFF_KR:OFF_FQ], p["g_kr"]), pos)
    q = (q_a @ p["w_qb"]).reshape(B, T, H_MLA, NOPE_DIM + ROPE_DIM)
    q_nope = rms_norm(q[..., :NOPE_DIM], p["g_qn"])
    q_rope = apply_rope(rms_norm(q[..., NOPE_DIM:], p["g_qr"]), pos)
    fq = rms_norm(z[..., OFF_FQ:OFF_FK].reshape(B, T, H_FOX, HD_FOX), p["g_fq"])
    fk = rms_norm(z[..., OFF_FK:OFF_FV].reshape(B, T, H_FOX, HD_FOX), p["g_fk"])
    fv = z[..., OFF_FV:OFF_FF].reshape(B, T, H_FOX, HD_FOX)
    log_f = jax.nn.log_sigmoid(z[..., OFF_FF:].astype(jnp.float32) + p["b_f"].astype(jnp.float32))
    return q_nope, q_rope, c_kv, k_rope, fq, fk, fv, log_f


def mla_expand(c_kv, w_kvb, g_kn):
    B, T, _ = c_kv.shape
    kv = (c_kv @ w_kvb).reshape(B, T, H_MLA, NOPE_DIM + V_DIM)
    return rms_norm(kv[..., :NOPE_DIM], g_kn), kv[..., NOPE_DIM:]


def attend(q_parts, k_parts, v, q_pos, k_pos, c_q, c_k, scale):
    s = None
    for qp, kp in zip(q_parts, k_parts):
        spec = "bqhd,bkd->bhqk" if kp.ndim == 3 else "bqhd,bkhd->bhqk"
        t = jnp.einsum(spec, qp, kp).astype(jnp.float32)
        s = t if s is None else s + t
    s = s * scale
    if c_q is not None:
        s = s + jnp.transpose(c_q, (0, 2, 1))[..., None] - jnp.transpose(c_k, (0, 2, 1))[:, :, None, :]
    mask = (k_pos[None, :] <= q_pos[:, None])[None, None]
    s = jnp.where(mask, s, NEG_INF)
    probs = jax.nn.softmax(s, axis=-1).astype(v.dtype)
    return jnp.einsum("bhqk,bkhd->bqhd", probs, v)


def prompt_attention(q_parts, k_parts, v, c, scale):
    B, S, H, dv = v.shape
    n_blk = S // QBLK
    k_pos = jnp.arange(S)

    def block(i):
        start = i * QBLK
        qs = tuple(lax.dynamic_slice_in_dim(qp, start, QBLK, axis=1) for qp in q_parts)
        cq = None if c is None else lax.dynamic_slice_in_dim(c, start, QBLK, axis=1)
        return attend(qs, k_parts, v, start + jnp.arange(QBLK), k_pos, cq, c, scale)

    out = lax.map(block, jnp.arange(n_blk))
    return jnp.swapaxes(out, 0, 1).reshape(B, S, H, dv)


def merge_heads(x, o_mla, o_fox, w_o):
    B, T = x.shape[:2]
    o = jnp.concatenate([o_mla.reshape(B, T, H_MLA * V_DIM), o_fox.reshape(B, T, H_FOX * HD_FOX)], axis=-1)
    return x + o @ w_o


def conv_ffn(x, conv_state, p):
    T = x.shape[1]
    u = rms_norm(x, p["g_ffn"]) @ p["w_up"]
    full = jnp.concatenate([conv_state.astype(u.dtype), u], axis=1)
    hc = p["b_conv"] + p["w_conv"][0] * full[:, 0:T]
    for k in range(1, CONV_W):
        hc = hc + p["w_conv"][k] * full[:, k:k + T]
    gate, val = hc[..., :D_FF], hc[..., D_FF:]
    y = (jax.nn.silu(gate) * val) @ p["w_down"]
    return x + y, full[:, full.shape[1] - (CONV_W - 1):]


def prompt_layer(x, p):
    B, S, _ = x.shape
    pos = jnp.arange(S)
    h = rms_norm(x, p["g_attn"])
    q_nope, q_rope, c_kv, k_rope, fq, fk, fv, log_f = mixer_inputs(h, pos, p)
    k_nope, v_mla = mla_expand(c_kv, p["w_kvb"], p["g_kn"])
    o_mla = prompt_attention((q_nope, q_rope), (k_nope, k_rope), v_mla, None, MLA_SCALE)
    c = jnp.cumsum(log_f.astype(jnp.float32), axis=1)
    o_fox = prompt_attention((fq,), (fk,), fv, c, FOX_SCALE)
    x1 = merge_heads(x, o_mla, o_fox, p["w_o"])
    zero_state = jnp.zeros((B, CONV_W - 1, 2 * D_FF), x.dtype)
    y, conv_new = conv_ffn(x1, zero_state, p)
    return y, c_kv, k_rope, fk, fv, log_f, conv_new


def sample_layer(x, caches, conv_state, page_table, p):
    cache_ckv, cache_krope, cache_fk, cache_fv, cache_logf = caches
    B, T, _ = x.shape
    P = page_table.shape[1] * cache_ckv.shape[1]
    pos = P + jnp.arange(T)
    k_pos = jnp.arange(P + T)
    h = rms_norm(x, p["g_attn"])
    q_nope, q_rope, c_kv, k_rope, fq, fk, fv, log_f = mixer_inputs(h, pos, p)

    def one_seq(args):
        pt, qn, qr, ckv_n, kr_n, fq_b, fk_n, fv_n, lf_n = args

        def with_past(cache, new):
            rows = cache[pt].reshape((P,) + cache.shape[2:])
            return jnp.concatenate([rows.astype(new.dtype), new], axis=0)[None]

        ckv_all = with_past(cache_ckv, ckv_n)
        kr_all = with_past(cache_krope, kr_n)
        fk_all = with_past(cache_fk, fk_n)
        fv_all = with_past(cache_fv, fv_n)
        lf_all = with_past(cache_logf, lf_n)
        kn_all, vm_all = mla_expand(ckv_all, p["w_kvb"], p["g_kn"])
        o_m = attend((qn[None], qr[None]), (kn_all, kr_all), vm_all, pos, k_pos, None, None, MLA_SCALE)
        c_all = jnp.cumsum(lf_all.astype(jnp.float32), axis=1)
        o_f = attend((fq_b[None],), (fk_all,), fv_all, pos, k_pos, c_all[:, P:], c_all, FOX_SCALE)
        return o_m[0], o_f[0]

    o_mla, o_fox = lax.map(one_seq, (page_table, q_nope, q_rope, c_kv, k_rope, fq, fk, fv, log_f))
    x1 = merge_heads(x, o_mla, o_fox, p["w_o"])
    y, conv_new = conv_ffn(x1, conv_state, p)
    return y, c_kv, k_rope, fk, fv, log_f, conv_new


def setup_inputs(seed: int = 0) -> dict:
    key = jax.random.key(seed)
    ks = jax.random.split(key, 32)
    n_pages = PAST_LEN // PAGE_SIZE
    n_used = DEC_BATCH * n_pages
    n_pool = n_used + max(1, n_used // 4)
    f32 = jnp.float32

    def nrm(k, shape, scale=1.0):
        return jax.random.normal(k, shape, f32) * scale

    def gain(k, n):
        return 1.0 + 0.1 * jax.random.normal(k, (n,), f32)

    perm = jax.random.permutation(ks[0], n_pool)
    page_table = perm[:n_used].reshape(DEC_BATCH, n_pages).astype(jnp.int32)
    return {
        "x_prompt": nrm(ks[1], (BATCH, SEQ, D_MODEL)),
        "x_sample": nrm(ks[2], (DEC_BATCH, DEC_SEQ, D_MODEL)),
        "cache_mla_ckv": nrm(ks[3], (n_pool, PAGE_SIZE, KV_LORA)),
        "cache_mla_krope": nrm(ks[4], (n_pool, PAGE_SIZE, ROPE_DIM)),
        "cache_fox_k": nrm(ks[5], (n_pool, PAGE_SIZE, H_FOX, HD_FOX)),
        "cache_fox_v": nrm(ks[6], (n_pool, PAGE_SIZE, H_FOX, HD_FOX)),
        "cache_fox_logf": jax.nn.log_sigmoid(3.0 + nrm(ks[7], (n_pool, PAGE_SIZE, H_FOX))),
        "state_ffn_conv": nrm(ks[8], (DEC_BATCH, CONV_W - 1, 2 * D_FF)),
        "page_table": page_table,
        "g_attn": gain(ks[9], D_MODEL),
        "w_in": nrm(ks[10], (D_MODEL, IN_COLS), D_MODEL ** -0.5),
        "b_f": 3.0 + 0.5 * jax.random.normal(ks[11], (H_FOX,), f32),
        "g_qa": gain(ks[12], Q_LORA),
        "w_qb": nrm(ks[13], (Q_LORA, H_MLA * (NOPE_DIM + ROPE_DIM)), Q_LORA ** -0.5),
        "g_kva": gain(ks[14], KV_LORA),
        "w_kvb": nrm(ks[15], (KV_LORA, H_MLA * (NOPE_DIM + V_DIM)), KV_LORA ** -0.5),
        "g_qn": gain(ks[16], NOPE_DIM),
        "g_qr": gain(ks[17], ROPE_DIM),
        "g_kn": gain(ks[18], NOPE_DIM),
        "g_kr": gain(ks[19], ROPE_DIM),
        "g_fq": gain(ks[20], HD_FOX),
        "g_fk": gain(ks[21], HD_FOX),
        "w_o": nrm(ks[22], (D_MIX, D_MODEL), D_MIX ** -0.5),
        "g_ffn": gain(ks[23], D_MODEL),
        "w_up": nrm(ks[24], (D_MODEL, 2 * D_FF), D_MODEL ** -0.5),
        "w_conv": nrm(ks[25], (CONV_W, 2 * D_FF), CONV_W ** -0.5),
        "b_conv": nrm(ks[26], (2 * D_FF,), 0.01),
        "w_down": nrm(ks[27], (D_FF, D_MODEL), D_FF ** -0.5),
    }


def reference(x_prompt, x_sample, cache_mla_ckv, cache_mla_krope, cache_fox_k, cache_fox_v,
              cache_fox_logf, state_ffn_conv, page_table, g_attn, w_in, b_f, g_qa, w_qb, g_kva,
              w_kvb, g_qn, g_qr, g_kn, g_kr, g_fq, g_fk, w_o, g_ffn, w_up, w_conv, b_conv, w_down):
    p = {"g_attn": g_attn, "w_in": w_in, "b_f": b_f, "g_qa": g_qa, "w_qb": w_qb, "g_kva": g_kva,
         "w_kvb": w_kvb, "g_qn": g_qn, "g_qr": g_qr, "g_kn": g_kn, "g_kr": g_kr, "g_fq": g_fq,
         "g_fk": g_fk, "w_o": w_o, "g_ffn": g_ffn, "w_up": w_up, "w_conv": w_conv,
         "b_conv": b_conv, "w_down": w_down}
    caches = (cache_mla_ckv, cache_mla_krope, cache_fox_k, cache_fox_v, cache_fox_logf)
    y_p, conv_p = x_prompt, None
    y_s, conv_s = x_sample, state_ffn_conv
    for _ in range(DEPTH):
        y_p, p_ckv, p_kr, p_fk, p_fv, p_lf, conv_p = prompt_layer(y_p, p)
        y_s, s_ckv, s_kr, s_fk, s_fv, s_lf, conv_s = sample_layer(y_s, caches, conv_s, page_table, p)
    return (y_p, y_s, p_ckv, p_kr, p_fk, p_fv, p_lf, conv_p, s_ckv, s_kr, s_fk, s_fv, s_lf, conv_s)
```

```python
import functools

import jax
import jax.numpy as jnp
import numpy as np
from jax import lax
from jax.experimental import pallas as pl
from jax.experimental.pallas import tpu as pltpu

F32 = jnp.float32
BF16 = jnp.bfloat16

LANES = 128
MXU_K = 256
N_HEADS = 8
NOPE_DIM = 64
ROPE_DIM = 32
HEAD_DIM = 64
Q_LORA = 256
KV_LORA = 128
ROPE_THETA = 10000.0
EPS = 1e-6
NEG_INF = -1e30
LOG2E = 1.4426950408889634
MLA_SCALE = (NOPE_DIM + ROPE_DIM) ** -0.5
FOX_SCALE = HEAD_DIM ** -0.5
ROPE_OFF = 64
VMEM_LIMIT = 56 * 1024 * 1024

C_QA, C_KVA, C_KR, C_FQ, C_FK, C_FV, C_FF, C_END = 0, 256, 384, 512, 1024, 1536, 2048, 2176

_NT = (((1,), (1,)), ((), ()))


def _lane(shape):
    return lax.broadcasted_iota(jnp.int32, shape, len(shape) - 1)


def _rms_rows(x, g):
    return x * lax.rsqrt(jnp.mean(x * x, axis=-1, keepdims=True) + EPS) * g


def _split3(x):
    hi = x.astype(BF16).astype(F32)
    mid = (x - hi).astype(BF16).astype(F32)
    lo = x - hi - mid
    return hi, mid, lo


def _rope_tables(pos, inv_lane, lane):
    ang = pos * inv_lane
    cos, sin = jnp.cos(ang), jnp.sin(ang)
    half = ROPE_DIM // 2
    is_rope = (lane >= ROPE_OFF) & (lane < ROPE_OFF + ROPE_DIM)
    first = (lane >= ROPE_OFF) & (lane < ROPE_OFF + half)
    second = (lane >= ROPE_OFF + half) & (lane < ROPE_OFF + ROPE_DIM)
    return jnp.where(is_rope, cos, 1.0), jnp.where(second, sin, 0.0), jnp.where(first, -sin, 0.0)


def _apply_rope(x, tables):
    c, s_from_first, s_from_second = tables
    half = ROPE_DIM // 2
    return (x * c + pltpu.roll(x, half, 1) * s_from_first
            + pltpu.roll(x, LANES - half, 1) * s_from_second)


def _pair_norm(x, g_pair, lane):
    lo = lane < HEAD_DIM
    x2 = x * x
    ss_lo = jnp.sum(jnp.where(lo, x2, 0.0), axis=-1, keepdims=True)
    ss_hi = jnp.sum(jnp.where(lo, 0.0, x2), axis=-1, keepdims=True)
    r = jnp.where(lo, lax.rsqrt(ss_lo / HEAD_DIM + EPS), lax.rsqrt(ss_hi / HEAD_DIM + EPS))
    return x * r * g_pair


def _log_sigmoid(x):
    return jnp.minimum(x, 0.0) - jnp.log1p(jnp.exp(-jnp.abs(x)))


def _proj_kernel(x_ref, gattn_ref, win_ref, gqa_ref, wqb_ref, gkva_ref, wkvb_ref, gq_ref, gk_ref,
                 gkr_ref, gfq_ref, gfk_ref, bf_ref, inv_ref, *refs, tm, pos_base, prompt):
    if prompt:
        (ckv_ref, kr_ref, fk_ref, fv_ref, lf_ref, qm_ref, km_ref, vm_ref, qf_ref, kf_ref, vf_ref,
         carry_ref) = refs
    else:
        ckv_ref, kr_ref, fk_ref, fv_ref, lf_ref, qm_ref, qf_ref = refs
    i = pl.program_id(1)
    lane = _lane((tm, LANES))

    x = x_ref[...]
    h = _rms_rows(x, gattn_ref[...])
    z = jnp.dot(h.astype(BF16), win_ref[...], preferred_element_type=F32)

    if prompt:
        pos = (pos_base + i * tm + lax.broadcasted_iota(jnp.int32, (tm, 1), 0)).astype(F32)
    else:
        pos = jnp.full((tm, 1), pos_base, F32)
    tables = _rope_tables(pos, inv_ref[...], lane)

    c_kv = _rms_rows(z[:, C_KVA:C_KR], gkva_ref[...])
    ckv_ref[...] = c_kv
    krz = z[:, C_KR:C_FQ]
    kr_ss = jnp.sum(krz * krz, axis=-1, keepdims=True)
    kr = _apply_rope(krz * lax.rsqrt(kr_ss / ROPE_DIM + EPS) * gkr_ref[...], tables)
    kr_ref[...] = kr

    q_a = _rms_rows(z[:, C_QA:C_KVA], gqa_ref[...])
    q = jnp.dot(q_a.astype(BF16), wqb_ref[...], preferred_element_type=F32)
    is_nope = lane < NOPE_DIM
    is_rope = (lane >= ROPE_OFF) & (lane < ROPE_OFF + ROPE_DIM)
    for hh in range(N_HEADS):
        qb = q[:, hh * LANES:(hh + 1) * LANES]
        q2 = qb * qb
        ss_n = jnp.sum(jnp.where(is_nope, q2, 0.0), axis=-1, keepdims=True)
        ss_r = jnp.sum(jnp.where(is_rope, q2, 0.0), axis=-1, keepdims=True)
        r = jnp.where(is_nope, lax.rsqrt(ss_n / NOPE_DIM + EPS), lax.rsqrt(ss_r / ROPE_DIM + EPS))
        qn = _apply_rope(qb * r * gq_ref[...], tables)
        qm_ref[hh] = (qn * (MLA_SCALE * LOG2E)).astype(qm_ref.dtype)

    fq_cols, fk_cols = [], []
    for j in range(N_HEADS // 2):
        sl = slice(j * LANES, (j + 1) * LANES)
        fq_cols.append(_pair_norm(z[:, C_FQ:C_FK][:, sl], gfq_ref[...], lane))
        fkn = _pair_norm(z[:, C_FK:C_FV][:, sl], gfk_ref[...], lane)
        fk_cols.append(fkn)
        fk_ref[:, sl] = fkn
    fv = z[:, C_FV:C_FF]
    fv_ref[...] = fv
    lf = _log_sigmoid(z[:, C_FF:C_END] + bf_ref[...])
    lf_ref[...] = lf

    if not prompt:
        for j in range(N_HEADS // 2):
            qf_ref[:, j * LANES:(j + 1) * LANES] = fq_cols[j] * (FOX_SCALE * LOG2E)
        return

    kv = jnp.dot(c_kv.astype(BF16), wkvb_ref[...], preferred_element_type=F32)
    for hh in range(N_HEADS):
        kb = kv[:, hh * LANES:(hh + 1) * LANES]
        ss = jnp.sum(kb * kb, axis=-1, keepdims=True)
        km_ref[hh] = (kb * lax.rsqrt(ss / NOPE_DIM + EPS) * gk_ref[...] + kr).astype(BF16)
    vm_ref[...] = kv[:, N_HEADS * LANES:].astype(BF16)
    vf_ref[...] = fv.astype(BF16)

    @pl.when(i == 0)
    def _():
        carry_ref[...] = jnp.zeros_like(carry_ref)

    lfm = jnp.where(lane < N_HEADS, lf, 0.0)
    parts = jnp.concatenate(_split3(lfm), axis=1).astype(BF16)
    r_i = lax.broadcasted_iota(jnp.int32, (tm, tm), 0)
    c_i = lax.broadcasted_iota(jnp.int32, (tm, tm), 1)
    tri = jnp.where(r_i >= c_i, 1.0, 0.0).astype(BF16)
    cs = jnp.dot(tri, parts, preferred_element_type=F32)
    c = cs[:, :LANES] + cs[:, LANES:2 * LANES] + cs[:, 2 * LANES:] + carry_ref[...]
    carry_ref[...] = c[tm - 1:tm, :]

    c_parts = _split3(c * LOG2E)
    for hh in range(N_HEADS):
        j, odd = hh // 2, hh % 2
        own = (lane >= HEAD_DIM) if odd else (lane < HEAD_DIM)
        l0 = 0 if odd else HEAD_DIM
        ch = [jnp.broadcast_to(p[:, hh:hh + 1], (tm, LANES)) for p in c_parts]
        qv = jnp.where(own, fq_cols[j] * (FOX_SCALE * LOG2E), 0.0)
        kvv = jnp.where(own, fk_cols[j], 0.0)
        for t in range(3):
            qv = jnp.where(lane == l0 + t, ch[t], qv)
            kvv = jnp.where(lane == l0 + 3 + t, -ch[t], kvv)
        qv = jnp.where((lane >= l0 + 3) & (lane < l0 + 6), 1.0, qv)
        kvv = jnp.where((lane >= l0) & (lane < l0 + 3), 1.0, kvv)
        qf_ref[hh] = qv.astype(BF16)
        kf_ref[hh] = kvv.astype(BF16)


def _project(x, w, *, prompt, pos_base, tm):
    b, s, d = x.shape
    grid = (b, s // tm)
    row = lambda n: pl.BlockSpec((None, tm, n), lambda bi, i: (bi, i, 0))
    head = pl.BlockSpec((None, N_HEADS, tm, LANES), lambda bi, i: (bi, 0, i, 0))
    full = lambda a: pl.BlockSpec(a.shape, lambda bi, i: (0,) * a.ndim)
    ins = [x, w["g_attn"], w["w_in"], w["g_qa"], w["w_qb"], w["g_kva"], w["w_kvb"], w["g_q"],
           w["g_k"], w["g_kr"], w["g_fq"], w["g_fk"], w["b_f"], w["inv_lane"]]
    in_specs = [row(d)] + [full(a) for a in ins[1:]]
    f32 = lambda n: jax.ShapeDtypeStruct((b, s, n), F32)
    out_shape = [f32(KV_LORA), f32(LANES), f32(N_HEADS * HEAD_DIM), f32(N_HEADS * HEAD_DIM), f32(LANES)]
    out_specs = [row(KV_LORA), row(LANES), row(N_HEADS * HEAD_DIM), row(N_HEADS * HEAD_DIM), row(LANES)]
    scratch = []
    if prompt:
        hb = jax.ShapeDtypeStruct((b, N_HEADS, s, LANES), BF16)
        vb = jax.ShapeDtypeStruct((b, s, N_HEADS * HEAD_DIM), BF16)
        out_shape += [hb, hb, vb, hb, hb, vb]
        out_specs += [head, head, row(N_HEADS * HEAD_DIM), head, head, row(N_HEADS * HEAD_DIM)]
        scratch = [pltpu.VMEM((1, LANES), F32)]
    else:
        out_shape += [jax.ShapeDtypeStruct((b, N_HEADS, s, LANES), F32), f32(N_HEADS * HEAD_DIM)]
        out_specs += [head, row(N_HEADS * HEAD_DIM)]
    return pl.pallas_call(
        functools.partial(_proj_kernel, tm=tm, pos_base=pos_base, prompt=prompt),
        grid=grid, in_specs=in_specs, out_specs=out_specs, out_shape=out_shape,
        scratch_shapes=scratch,
        compiler_params=pltpu.CompilerParams(
            dimension_semantics=("parallel", "arbitrary"), vmem_limit_bytes=VMEM_LIMIT),
        name="proj_prompt" if prompt else "proj_sample",
    )(*ins)


def _attn_kernel(qi_ref, ki_ref, q_ref, k_ref, v_ref, o_ref, m_ref, l_ref, acc_ref, *, tq):
    p = pl.program_id(2)
    qi, ki = qi_ref[p], ki_ref[p]

    @pl.when(ki == 0)
    def _():
        m_ref[...] = jnp.full_like(m_ref, NEG_INF)
        l_ref[...] = jnp.zeros_like(l_ref)
        acc_ref[...] = jnp.zeros_like(acc_ref)

    r_i = lax.broadcasted_iota(jnp.int32, (tq, tq), 0)
    c_i = lax.broadcasted_iota(jnp.int32, (tq, tq), 1)
    visible = (ki < qi) | (r_i >= c_i)
    v = v_ref[...]
    for hh in range(2):
        s = lax.dot_general(q_ref[hh], k_ref[hh], _NT, preferred_element_type=F32)
        s = jnp.where(visible, s, NEG_INF)
        m_prev = m_ref[hh]
        m_new = jnp.maximum(m_prev, jnp.max(s, axis=-1, keepdims=True))
        alpha = jnp.exp2(m_prev - m_new)
        pr = jnp.exp2(s - m_new)
        l_ref[hh] = alpha * l_ref[hh] + jnp.sum(pr, axis=-1, keepdims=True)
        acc_ref[hh] = alpha * acc_ref[hh] + jnp.dot(pr.astype(BF16), v, preferred_element_type=F32)
        m_ref[hh] = m_new

    @pl.when(ki == qi)
    def _():
        lane = _lane((tq, LANES))
        o_ref[...] = jnp.where(lane < HEAD_DIM, acc_ref[0] / l_ref[0], acc_ref[1] / l_ref[1])


def _prompt_attention(q, k, v, *, tq):
    b, _, s, _ = q.shape
    nq = s // tq
    pairs = [(a, c) for a in range(nq) for c in range(a + 1)]
    qi = jnp.asarray(np.array([a for a, _ in pairs], np.int32))
    ki = jnp.asarray(np.array([c for _, c in pairs], np.int32))
    grid_spec = pltpu.PrefetchScalarGridSpec(
        num_scalar_prefetch=2,
        grid=(b, N_HEADS // 2, len(pairs)),
        in_specs=[
            pl.BlockSpec((None, 2, tq, LANES), lambda bi, hp, p, qi, ki: (bi, hp, qi[p], 0)),
            pl.BlockSpec((None, 2, tq, LANES), lambda bi, hp, p, qi, ki: (bi, hp, ki[p], 0)),
            pl.BlockSpec((None, tq, LANES), lambda bi, hp, p, qi, ki: (bi, ki[p], hp)),
        ],
        out_specs=pl.BlockSpec((None, tq, LANES), lambda bi, hp, p, qi, ki: (bi, qi[p], hp)),
        scratch_shapes=[pltpu.VMEM((2, tq, 1), F32), pltpu.VMEM((2, tq, 1), F32),
                        pltpu.VMEM((2, tq, LANES), F32)],
    )
    return pl.pallas_call(
        functools.partial(_attn_kernel, tq=tq),
        grid_spec=grid_spec,
        out_shape=jax.ShapeDtypeStruct((b, s, N_HEADS * HEAD_DIM), F32),
        compiler_params=pltpu.CompilerParams(
            dimension_semantics=("parallel", "parallel", "arbitrary"), vmem_limit_bytes=VMEM_LIMIT),
        name="prompt_attention",
    )(qi, ki, q, k, v)


def _page_copies(pt_ref, seq, chunk, slot, n_pages, streams):
    out = []
    for j in range(n_pages):
        page = pt_ref[seq, chunk * n_pages + j]
        for hbm, buf, rows, sem in streams:
            out.append(pltpu.make_async_copy(
                hbm.at[page], buf.at[slot, pl.ds(j * rows, rows)], sem.at[slot]))
    return out


def _gather_step(pt_ref, n_chunks, n_pages, streams, chunk_of):
    b, c = pl.program_id(0), pl.program_id(1)
    step = b * n_chunks + c
    slot = step % 2
    total = pl.num_programs(0) * n_chunks

    @pl.when(step == 0)
    def _():
        for cp in _page_copies(pt_ref, b, chunk_of(c), slot, n_pages, streams):
            cp.start()

    @pl.when(step + 1 < total)
    def _():
        last = c == n_chunks - 1
        nb = jnp.where(last, b + 1, b)
        nc = jnp.where(last, 0, c + 1)
        for cp in _page_copies(pt_ref, nb, chunk_of(nc), 1 - slot, n_pages, streams):
            cp.start()

    for cp in _page_copies(pt_ref, b, chunk_of(c), slot, n_pages, streams):
        cp.wait()
    return slot


def _dec_mla_kernel(pt_ref, q_ref, cnew_ref, krnew_ref, wkabs_ref, wk_ref, wv_ref, gk_ref,
                    ckv_hbm, kr_hbm, o_ref, cbuf, kbuf, sem, m_ref, l_ref, acc_ref, qlat_ref,
                    *, n_chunks, n_pages, page):
    c = pl.program_id(1)
    streams = [(ckv_hbm, cbuf, page, sem.at[0]), (kr_hbm, kbuf, page, sem.at[1])]
    slot = _gather_step(pt_ref, n_chunks, n_pages, streams, lambda cc: cc)
    sub = lax.broadcasted_iota(jnp.int32, (N_HEADS, 1), 0)

    @pl.when(c == 0)
    def _():
        m_ref[...] = jnp.full_like(m_ref, NEG_INF)
        l_ref[...] = jnp.zeros_like(l_ref)
        acc_ref[...] = jnp.zeros_like(acc_ref)
        qg = (q_ref[...] * gk_ref[...]).astype(BF16)
        g_all = jnp.dot(qg, wkabs_ref[...], preferred_element_type=F32)
        q_lat = jnp.zeros((N_HEADS, KV_LORA), F32)
        for hh in range(N_HEADS):
            q_lat = q_lat + jnp.where(sub == hh, g_all[:, hh * KV_LORA:(hh + 1) * KV_LORA], 0.0)
        qlat_ref[...] = q_lat

    lane512 = _lane((N_HEADS, N_HEADS * NOPE_DIM))
    seg = jnp.where(lane512 // NOPE_DIM == sub, 1.0, 0.0).astype(BF16)
    q_lat = qlat_ref[...].astype(BF16)
    q_rope = q_ref[...][:, ROPE_OFF:ROPE_OFF + ROPE_DIM].astype(BF16)

    def scores(c_rows, kr_rows):
        cb = c_rows.astype(BF16)
        kk = jnp.dot(cb, wk_ref[...], preferred_element_type=F32)
        ssq = lax.dot_general(seg, (kk * kk).astype(BF16), _NT, preferred_element_type=F32)
        s_n = lax.dot_general(q_lat, cb, _NT, preferred_element_type=F32)
        s_r = lax.dot_general(q_rope, kr_rows.astype(BF16), _NT, preferred_element_type=F32)
        return s_n * lax.rsqrt(ssq / NOPE_DIM + EPS) + s_r, cb

    def update(s, cb):
        m_prev = m_ref[...]
        m_new = jnp.maximum(m_prev, jnp.max(s, axis=-1, keepdims=True))
        alpha = jnp.exp2(m_prev - m_new)
        pr = jnp.exp2(s - m_new)
        l_ref[...] = alpha * l_ref[...] + jnp.sum(pr, axis=-1, keepdims=True)
        acc_ref[...] = alpha * acc_ref[...] + jnp.dot(pr.astype(BF16), cb, preferred_element_type=F32)
        m_ref[...] = m_new

    update(*scores(cbuf[slot], kbuf[slot]))

    @pl.when(c == n_chunks - 1)
    def _():
        s, cb = scores(cnew_ref[...], krnew_ref[...])
        s = jnp.where(_lane(s.shape) == 0, s, NEG_INF)
        update(s, cb)
        o_lat = (acc_ref[...] / l_ref[...]).astype(BF16)
        o_all = jnp.dot(o_lat, wv_ref[...], preferred_element_type=F32)
        o_ref[...] = jnp.sum(jnp.where(lane512 // HEAD_DIM == sub, o_all, 0.0), axis=0, keepdims=True)


def _decode_mla(page_table, q, c_new, kr_new, w, cache_ckv, cache_kr, *, n_pages):
    b, total_pages = page_table.shape
    page = cache_ckv.shape[1]
    n_chunks = total_pages // n_pages
    t = n_pages * page
    full = lambda a: pl.BlockSpec(a.shape, lambda bi, ci, pt: (0,) * a.ndim)
    seq = lambda a: pl.BlockSpec((None,) + a.shape[1:], lambda bi, ci, pt: (bi,) + (0,) * (a.ndim - 1))
    consts = [w["wk_abs"], w["wk"], w["wv"], w["g_k"]]
    grid_spec = pltpu.PrefetchScalarGridSpec(
        num_scalar_prefetch=1,
        grid=(b, n_chunks),
        in_specs=[seq(q), seq(c_new), seq(kr_new)] + [full(a) for a in consts]
        + [pl.BlockSpec(memory_space=pl.ANY), pl.BlockSpec(memory_space=pl.ANY)],
        out_specs=pl.BlockSpec((None, 1, N_HEADS * HEAD_DIM), lambda bi, ci, pt: (bi, 0, 0)),
        scratch_shapes=[
            pltpu.VMEM((2, t, KV_LORA), F32), pltpu.VMEM((2, t, ROPE_DIM), F32),
            pltpu.SemaphoreType.DMA((2, 2)),
            pltpu.VMEM((N_HEADS, 1), F32), pltpu.VMEM((N_HEADS, 1), F32),
            pltpu.VMEM((N_HEADS, KV_LORA), F32), pltpu.VMEM((N_HEADS, KV_LORA), F32)],
    )
    return pl.pallas_call(
        functools.partial(_dec_mla_kernel, n_chunks=n_chunks, n_pages=n_pages, page=page),
        grid_spec=grid_spec,
        out_shape=jax.ShapeDtypeStruct((b, 1, N_HEADS * HEAD_DIM), F32),
        compiler_params=pltpu.CompilerParams(
            dimension_semantics=("arbitrary", "arbitrary"), vmem_limit_bytes=VMEM_LIMIT),
        name="decode_mla",
    )(page_table, q, c_new, kr_new, *consts, cache_ckv, cache_kr)


def _dot_f32(a, b):
    return jnp.dot(a, b, precision=lax.Precision.HIGHEST, preferred_element_type=F32)


def _dec_fox_kernel(pt_ref, q_ref, knew_ref, vnew_ref, lfnew_ref, k_hbm, v_hbm, lf_hbm, o_ref,
                    kbuf, vbuf, lbuf, sem, m_ref, l_ref, acc_ref, carry_ref,
                    *, n_chunks, n_pages, page):
    c = pl.program_id(1)
    rows_per_page = page * N_HEADS // LANES
    streams = [(k_hbm, kbuf, page, sem.at[0]), (v_hbm, vbuf, page, sem.at[1]),
               (lf_hbm, lbuf, rows_per_page, sem.at[2])]
    slot = _gather_step(pt_ref, n_chunks, n_pages, streams, lambda cc: n_chunks - 1 - cc)
    q = q_ref[...]

    @pl.when(c == 0)
    def _():
        m_ref[...] = jnp.sum(q * knew_ref[...], axis=-1, keepdims=True)
        l_ref[...] = jnp.ones_like(l_ref)
        acc_ref[...] = vnew_ref[...]
        carry_ref[...] = lfnew_ref[...]

    t = n_pages * page
    n_rows = n_pages * rows_per_page
    kb = kbuf[slot].reshape(t * N_HEADS, HEAD_DIM).astype(BF16)
    vb = vbuf[slot].reshape(t * N_HEADS, HEAD_DIM).astype(BF16)
    s_all = lax.dot_general(q.astype(BF16), kb, _NT, preferred_element_type=F32)

    lf = lbuf[slot]
    a_i = lax.broadcasted_iota(jnp.int32, (LANES, LANES), 0)
    b_i = lax.broadcasted_iota(jnp.int32, (LANES, LANES), 1)
    same_head = (a_i % N_HEADS) == (b_i % N_HEADS)
    later_in_row = jnp.where(same_head & (a_i // N_HEADS > b_i // N_HEADS), 1.0, 0.0)
    head_total = jnp.where(same_head, 1.0, 0.0)
    r_i = lax.broadcasted_iota(jnp.int32, (n_rows, n_rows), 0)
    c_i = lax.broadcasted_iota(jnp.int32, (n_rows, n_rows), 1)
    later_rows = jnp.where(c_i > r_i, 1.0, 0.0)
    row_tot = _dot_f32(lf, head_total)
    bias = _dot_f32(lf, later_in_row) + _dot_f32(later_rows, row_tot) + carry_ref[...]
    carry_ref[...] = carry_ref[...] + jnp.sum(row_tot, axis=0, keepdims=True)
    bias = bias * LOG2E

    lane = _lane((N_HEADS, LANES))
    sub = lax.broadcasted_iota(jnp.int32, (N_HEADS, LANES), 0)
    own = (lane % N_HEADS) == sub
    logits = []
    for j in range(n_rows):
        blk = s_all[:, j * LANES:(j + 1) * LANES] + bias[j:j + 1, :]
        logits.append(jnp.where(own, blk, NEG_INF))
    lg = jnp.concatenate(logits, axis=1)
    m_prev = m_ref[...]
    m_new = jnp.maximum(m_prev, jnp.max(lg, axis=-1, keepdims=True))
    alpha = jnp.exp2(m_prev - m_new)
    pr = jnp.exp2(lg - m_new)
    l_ref[...] = alpha * l_ref[...] + jnp.sum(pr, axis=-1, keepdims=True)
    acc_ref[...] = alpha * acc_ref[...] + jnp.dot(pr.astype(BF16), vb, preferred_element_type=F32)
    m_ref[...] = m_new

    @pl.when(c == n_chunks - 1)
    def _():
        o_ref[...] = acc_ref[...] / l_ref[...]


def _decode_fox(page_table, q, k_new, v_new, lf_new, cache_k, cache_v, cache_lf_flat, *, n_pages):
    b, total_pages = page_table.shape
    page = cache_k.shape[1]
    n_chunks = total_pages // n_pages
    t = n_pages * page
    seq = lambda a: pl.BlockSpec((None,) + a.shape[1:], lambda bi, ci, pt: (bi,) + (0,) * (a.ndim - 1))
    hbm = pl.BlockSpec(memory_space=pl.ANY)
    grid_spec = pltpu.PrefetchScalarGridSpec(
        num_scalar_prefetch=1,
        grid=(b, n_chunks),
        in_specs=[seq(q), seq(k_new), seq(v_new), seq(lf_new), hbm, hbm, hbm],
        out_specs=pl.BlockSpec((None, N_HEADS, HEAD_DIM), lambda bi, ci, pt: (bi, 0, 0)),
        scratch_shapes=[
            pltpu.VMEM((2, t, N_HEADS, HEAD_DIM), F32), pltpu.VMEM((2, t, N_HEADS, HEAD_DIM), F32),
            pltpu.VMEM((2, t * N_HEADS // LANES, LANES), F32),
            pltpu.SemaphoreType.DMA((3, 2)),
            pltpu.VMEM((N_HEADS, 1), F32), pltpu.VMEM((N_HEADS, 1), F32),
            pltpu.VMEM((N_HEADS, HEAD_DIM), F32), pltpu.VMEM((1, LANES), F32)],
    )
    return pl.pallas_call(
        functools.partial(_dec_fox_kernel, n_chunks=n_chunks, n_pages=n_pages, page=page),
        grid_spec=grid_spec,
        out_shape=jax.ShapeDtypeStruct((b, N_HEADS, HEAD_DIM), F32),
        compiler_params=pltpu.CompilerParams(
            dimension_semantics=("arbitrary", "arbitrary"), vmem_limit_bytes=VMEM_LIMIT),
        name="decode_fox",
    )(page_table, q, k_new, v_new, lf_new, cache_k, cache_v, cache_lf_flat)


def _post_kernel(x_ref, om_ref, of_ref, wo_ref, gffn_ref, wup_ref, wconv_ref, bconv_ref, wdown_ref,
                 *refs, tm, d_ff, bounds, prompt):
    if prompt:
        y_ref, tail_ref, ubuf = refs
    else:
        s0_ref, s1_ref, y_ref, u_ref = refs
    half = N_HEADS * HEAD_DIM
    x1 = (x_ref[...]
          + jnp.dot(om_ref[...].astype(BF16), wo_ref[:half, :], preferred_element_type=F32)
          + jnp.dot(of_ref[...].astype(BF16), wo_ref[half:, :], preferred_element_type=F32))
    h2 = _rms_rows(x1, gffn_ref[...]).astype(BF16)

    if prompt:
        @pl.when(pl.program_id(1) == 0)
        def _():
            ubuf[0:8, :] = jnp.zeros((8, 2 * d_ff), F32)

    acts = []
    for lo, hi in zip(bounds[:-1], bounds[1:]):
        hc = []
        for base in (0, d_ff):
            cols = slice(base + lo, base + hi)
            u = jnp.dot(h2, wup_ref[:, cols], preferred_element_type=F32)
            if prompt:
                ubuf[8:8 + tm, cols] = u
                u1, u2 = ubuf[7:7 + tm, cols], ubuf[6:6 + tm, cols]
                tail = ubuf[tm:tm + 8, cols]
                ubuf[0:8, cols] = tail
                tail_ref[:, cols] = tail
            else:
                u_ref[:, cols] = u
                u1, u2 = s1_ref[:, cols], s0_ref[:, cols]
            hc.append(bconv_ref[:, cols] + wconv_ref[0:1, cols] * u2 + wconv_ref[1:2, cols] * u1
                      + wconv_ref[2:3, cols] * u)
        gate, val = hc
        acts.append((gate * (1.0 / (1.0 + jnp.exp(-gate))) * val).astype(BF16))
    act = jnp.concatenate(acts, axis=1)
    y_ref[...] = x1 + jnp.dot(act, wdown_ref[...], preferred_element_type=F32)


def _ffn_bounds(d_ff):
    tiles = d_ff // MXU_K
    assert tiles * MXU_K == d_ff
    return (0, (tiles + 1) // 2 * MXU_K, d_ff)


def _post(x, o_mla, o_fox, w, state, *, prompt, tm):
    b, s, d = x.shape
    d_ff = w["w_down"].shape[0]
    grid = (b, s // tm)
    row = lambda n: pl.BlockSpec((None, tm, n), lambda bi, i: (bi, i, 0))
    full = lambda a: pl.BlockSpec(a.shape, lambda bi, i: (0,) * a.ndim)
    consts = [w["w_o"], w["g_ffn"], w["w_up"], w["w_conv"], w["b_conv"], w["w_down"]]
    ins = [x, o_mla, o_fox] + consts
    in_specs = [row(d), row(o_mla.shape[-1]), row(o_fox.shape[-1])] + [full(a) for a in consts]
    out_shape = [jax.ShapeDtypeStruct((b, s, d), F32)]
    out_specs = [row(d)]
    scratch = []
    if prompt:
        out_shape.append(jax.ShapeDtypeStruct((b, 8, 2 * d_ff), F32))
        out_specs.append(pl.BlockSpec((None, 8, 2 * d_ff), lambda bi, i: (bi, 0, 0)))
        scratch = [pltpu.VMEM((tm + 8, 2 * d_ff), F32)]
    else:
        ins += [state[0], state[1]]
        in_specs += [row(2 * d_ff), row(2 * d_ff)]
        out_shape.append(jax.ShapeDtypeStruct((b, s, 2 * d_ff), F32))
        out_specs.append(row(2 * d_ff))
    return pl.pallas_call(
        functools.partial(_post_kernel, tm=tm, d_ff=d_ff, bounds=_ffn_bounds(d_ff), prompt=prompt),
        grid=grid, in_specs=in_specs, out_specs=out_specs, out_shape=out_shape,
        scratch_shapes=scratch,
        compiler_params=pltpu.CompilerParams(
            dimension_semantics=("parallel", "arbitrary"), vmem_limit_bytes=VMEM_LIMIT),
        name="post_prompt" if prompt else "post_sample",
    )(*ins)


def _pad_cols(a, width, offset=0):
    return jnp.pad(a, ((0, 0), (offset, width - offset - a.shape[1])))


def _prepare_weights(g_attn, w_in, b_f, g_qa, w_qb, g_kva, w_kvb, g_qn, g_qr, g_kn, g_kr, g_fq,
                     g_fk, w_o, g_ffn, w_up, w_conv, b_conv, w_down):
    hd = N_HEADS * HEAD_DIM
    o_kva, o_kr = Q_LORA, Q_LORA + KV_LORA
    o_fq = o_kr + ROPE_DIM
    o_fk, o_fv, o_ff = o_fq + hd, o_fq + 2 * hd, o_fq + 3 * hd
    w_in_p = jnp.concatenate([
        w_in[:, :o_kr],
        _pad_cols(w_in[:, o_kr:o_fq], LANES, ROPE_OFF),
        w_in[:, o_fq:o_ff],
        _pad_cols(w_in[:, o_ff:], LANES)], axis=1).astype(BF16)
    d_q = NOPE_DIM + ROPE_DIM
    w_qb_p = jnp.pad(w_qb.reshape(Q_LORA, N_HEADS, d_q), ((0, 0), (0, 0), (0, LANES - d_q)))
    w_qb_p = w_qb_p.reshape(Q_LORA, N_HEADS * LANES).astype(BF16)
    w_kvb3 = w_kvb.reshape(KV_LORA, N_HEADS, NOPE_DIM + HEAD_DIM)
    wk3, wv3 = w_kvb3[:, :, :NOPE_DIM], w_kvb3[:, :, NOPE_DIM:]
    wk_pad = jnp.pad(wk3, ((0, 0), (0, 0), (0, LANES - NOPE_DIM))).reshape(KV_LORA, N_HEADS * LANES)
    wv = wv3.reshape(KV_LORA, hd)
    wk_abs = jnp.pad(jnp.transpose(wk3, (2, 1, 0)), ((0, LANES - NOPE_DIM), (0, 0), (0, 0)))
    half = ROPE_DIM // 2
    inv = ROPE_THETA ** (-jnp.arange(half, dtype=F32) / half)
    inv_lane = _pad_cols(jnp.tile(inv, 2)[None, :], LANES, ROPE_OFF)
    row = lambda a: a[None, :].astype(F32)
    return {
        "g_attn": row(g_attn), "w_in": w_in_p, "g_qa": row(g_qa), "w_qb": w_qb_p,
        "g_kva": row(g_kva), "w_kvb": jnp.concatenate([wk_pad, wv], axis=1).astype(BF16),
        "g_q": _pad_cols(jnp.concatenate([g_qn, g_qr])[None, :], LANES),
        "g_k": _pad_cols(row(g_kn), LANES), "g_kr": _pad_cols(row(g_kr), LANES, ROPE_OFF),
        "g_fq": row(jnp.tile(g_fq, 2)), "g_fk": row(jnp.tile(g_fk, 2)),
        "b_f": _pad_cols(row(b_f), LANES), "inv_lane": inv_lane,
        "wk_abs": wk_abs.reshape(LANES, N_HEADS * KV_LORA).astype(BF16),
        "wk": wk3.reshape(KV_LORA, hd).astype(BF16), "wv": wv.astype(BF16),
        "w_o": w_o.astype(BF16), "g_ffn": row(g_ffn), "w_up": w_up.astype(BF16),
        "w_conv": w_conv.astype(F32), "b_conv": row(b_conv), "w_down": w_down.astype(BF16),
    }


def _tile(n, pref):
    t = min(n, pref)
    assert n % t == 0, (n, pref)
    return t


def kernel(x_prompt, x_sample, cache_mla_ckv, cache_mla_krope, cache_fox_k, cache_fox_v, cache_fox_logf, state_ffn_conv, page_table, g_attn, w_in, b_f, g_qa, w_qb, g_kva, w_kvb, g_qn, g_qr, g_kn, g_kr, g_fq, g_fk, w_o, g_ffn, w_up, w_conv, b_conv, w_down):
    w = _prepare_weights(g_attn, w_in, b_f, g_qa, w_qb, g_kva, w_kvb, g_qn, g_qr, g_kn, g_kr,
                         g_fq, g_fk, w_o, g_ffn, w_up, w_conv, b_conv, w_down)
    bp, s, _ = x_prompt.shape
    bd, t_dec, d = x_sample.shape
    assert t_dec == 1
    n_pool, page = cache_mla_ckv.shape[:2]
    past = page_table.shape[1] * page
    hd = N_HEADS * HEAD_DIM

    (p_ckv, p_kr, p_fk, p_fv, p_lf, qm, km, vm, qf, kf, vf) = _project(
        x_prompt, w, prompt=True, pos_base=0, tm=_tile(s, 256))
    tq = _tile(s, 512)
    o_mla = _prompt_attention(qm, km, vm, tq=tq)
    o_fox = _prompt_attention(qf, kf, vf, tq=tq)
    y_p, tail = _post(x_prompt, o_mla, o_fox, w, None, prompt=True, tm=_tile(s, 256))

    xs = x_sample.reshape(1, bd, d)
    s_ckv, s_kr, s_fk, s_fv, s_lf, qs, qfs = _project(
        xs, w, prompt=False, pos_base=past, tm=_tile(bd, 128))
    s_kr32 = s_kr[0, :, ROPE_OFF:ROPE_OFF + ROPE_DIM]
    pad8 = lambda a: jnp.pad(a[:, None, :], ((0, 0), (0, 7), (0, 0)))
    o_mla_s = _decode_mla(
        page_table, jnp.transpose(qs[0], (1, 0, 2)), pad8(s_ckv[0]), pad8(s_kr32), w,
        cache_mla_ckv, cache_mla_krope, n_pages=_tile(page_table.shape[1], 16))
    lf_flat = cache_fox_logf.reshape(n_pool, page * N_HEADS // LANES, LANES)
    lf_new = jnp.tile(s_lf[0, :, :N_HEADS], (1, LANES // N_HEADS))[:, None, :]
    o_fox_s = _decode_fox(
        page_table, qfs[0].reshape(bd, N_HEADS, HEAD_DIM), s_fk[0].reshape(bd, N_HEADS, HEAD_DIM),
        s_fv[0].reshape(bd, N_HEADS, HEAD_DIM), lf_new, cache_fox_k, cache_fox_v, lf_flat,
        n_pages=_tile(page_table.shape[1], 8))
    state = (state_ffn_conv[:, 0, :][None], state_ffn_conv[:, 1, :][None])
    y_s, u_s = _post(xs, o_mla_s.reshape(1, bd, hd), o_fox_s.reshape(1, bd, hd), w, state,
                     prompt=False, tm=_tile(bd, 128))

    return (
        y_p, y_s.reshape(bd, 1, d),
        p_ckv, p_kr[:, :, ROPE_OFF:ROPE_OFF + ROPE_DIM],
        p_fk.reshape(bp, s, N_HEADS, HEAD_DIM), p_fv.reshape(bp, s, N_HEADS, HEAD_DIM),
        p_lf[:, :, :N_HEADS], tail[:, 6:8, :],
        s_ckv.reshape(bd, 1, KV_LORA), s_kr32.reshape(bd, 1, ROPE_DIM),
        s_fk.reshape(bd, 1, N_HEADS, HEAD_DIM), s_fv.reshape(bd, 1, N_HEADS, HEAD_DIM),
        s_lf[0, :, :N_HEADS].reshape(bd, 1, N_HEADS),
        jnp.stack([state_ffn_conv[:, 1, :], u_s[0]], axis=1),
    )
```

```python
import functools

import jax
import jax.numpy as jnp
import numpy as np
from jax import lax
from jax.experimental import pallas as pl
from jax.experimental.pallas import tpu as pltpu

F32 = jnp.float32
BF16 = jnp.bfloat16

LANES = 128
MXU_K = 256
N_HEADS = 8
NOPE_DIM = 64
ROPE_DIM = 32
HEAD_DIM = 64
Q_LORA = 256
KV_LORA = 128
ROPE_THETA = 10000.0
EPS = 1e-6
NEG_INF = -1e30
LOG2E = 1.4426950408889634
MLA_SCALE = (NOPE_DIM + ROPE_DIM) ** -0.5
FOX_SCALE = HEAD_DIM ** -0.5
ROPE_OFF = 64
VMEM_LIMIT = 56 * 1024 * 1024

C_QA, C_KVA, C_KR, C_FQ, C_FK, C_FV, C_FF, C_END = 0, 256, 384, 512, 1024, 1536, 2048, 2176

_NT = (((1,), (1,)), ((), ()))


def _lane(shape):
    return lax.broadcasted_iota(jnp.int32, shape, len(shape) - 1)


def _rms_rows(x, g):
    return x * lax.rsqrt(jnp.mean(x * x, axis=-1, keepdims=True) + EPS) * g


def _split3(x):
    hi = x.astype(BF16).astype(F32)
    mid = (x - hi).astype(BF16).astype(F32)
    lo = x - hi - mid
    return hi, mid, lo


def _rope_tables(pos, inv_lane, lane):
    ang = pos * inv_lane
    cos, sin = jnp.cos(ang), jnp.sin(ang)
    half = ROPE_DIM // 2
    is_rope = (lane >= ROPE_OFF) & (lane < ROPE_OFF + ROPE_DIM)
    first = (lane >= ROPE_OFF) & (lane < ROPE_OFF + half)
    second = (lane >= ROPE_OFF + half) & (lane < ROPE_OFF + ROPE_DIM)
    return jnp.where(is_rope, cos, 1.0), jnp.where(second, sin, 0.0), jnp.where(first, -sin, 0.0)


def _apply_rope(x, tables):
    c, s_from_first, s_from_second = tables
    half = ROPE_DIM // 2
    return (x * c + pltpu.roll(x, half, 1) * s_from_first
            + pltpu.roll(x, LANES - half, 1) * s_from_second)


def _pair_norm(x, g_pair, lane):
    lo = lane < HEAD_DIM
    x2 = x * x
    ss_lo = jnp.sum(jnp.where(lo, x2, 0.0), axis=-1, keepdims=True)
    ss_hi = jnp.sum(jnp.where(lo, 0.0, x2), axis=-1, keepdims=True)
    r = jnp.where(lo, lax.rsqrt(ss_lo / HEAD_DIM + EPS), lax.rsqrt(ss_hi / HEAD_DIM + EPS))
    return x * r * g_pair


def _log_sigmoid(x):
    return jnp.minimum(x, 0.0) - jnp.log1p(jnp.exp(-jnp.abs(x)))


def _proj_kernel(x_ref, gattn_ref, win_ref, gqa_ref, wqb_ref, gkva_ref, wkvb_ref, gq_ref, gk_ref,
                 gkr_ref, gfq_ref, gfk_ref, bf_ref, inv_ref, *refs, tm, pos_base, prompt):
    if prompt:
        (ckv_ref, kr_ref, fk_ref, fv_ref, lf_ref, qm_ref, km_ref, vm_ref, qf_ref, kf_ref, vf_ref,
         carry_ref) = refs
    else:
        ckv_ref, kr_ref, fk_ref, fv_ref, lf_ref, qm_ref, qf_ref = refs
    i = pl.program_id(1)
    lane = _lane((tm, LANES))

    x = x_ref[...]
    h = _rms_rows(x, gattn_ref[...])
    z = jnp.dot(h.astype(BF16), win_ref[...], preferred_element_type=F32)

    if prompt:
        pos = (pos_base + i * tm + lax.broadcasted_iota(jnp.int32, (tm, 1), 0)).astype(F32)
    else:
        pos = jnp.full((tm, 1), pos_base, F32)
    tables = _rope_tables(pos, inv_ref[...], lane)

    c_kv = _rms_rows(z[:, C_KVA:C_KR], gkva_ref[...])
    ckv_ref[...] = c_kv
    krz = z[:, C_KR:C_FQ]
    kr_ss = jnp.sum(krz * krz, axis=-1, keepdims=True)
    kr = _apply_rope(krz * lax.rsqrt(kr_ss / ROPE_DIM + EPS) * gkr_ref[...], tables)
    kr_ref[...] = kr

    q_a = _rms_rows(z[:, C_QA:C_KVA], gqa_ref[...])
    q = jnp.dot(q_a.astype(BF16), wqb_ref[...], preferred_element_type=F32)
    is_nope = lane < NOPE_DIM
    is_rope = (lane >= ROPE_OFF) & (lane < ROPE_OFF + ROPE_DIM)
    for hh in range(N_HEADS):
        qb = q[:, hh * LANES:(hh + 1) * LANES]
        q2 = qb * qb
        ss_n = jnp.sum(jnp.where(is_nope, q2, 0.0), axis=-1, keepdims=True)
        ss_r = jnp.sum(jnp.where(is_rope, q2, 0.0), axis=-1, keepdims=True)
        r = jnp.where(is_nope, lax.rsqrt(ss_n / NOPE_DIM + EPS), lax.rsqrt(ss_r / ROPE_DIM + EPS))
        qn = _apply_rope(qb * r * gq_ref[...], tables)
        qm_ref[hh] = (qn * (MLA_SCALE * LOG2E)).astype(qm_ref.dtype)

    fq_cols, fk_cols = [], []
    for j in range(N_HEADS // 2):
        sl = slice(j * LANES, (j + 1) * LANES)
        fq_cols.append(_pair_norm(z[:, C_FQ:C_FK][:, sl], gfq_ref[...], lane))
        fkn = _pair_norm(z[:, C_FK:C_FV][:, sl], gfk_ref[...], lane)
        fk_cols.append(fkn)
        fk_ref[:, sl] = fkn
    fv = z[:, C_FV:C_FF]
    fv_ref[...] = fv
    lf = _log_sigmoid(z[:, C_FF:C_END] + bf_ref[...])
    lf_ref[...] = lf

    if not prompt:
        for j in range(N_HEADS // 2):
            qf_ref[:, j * LANES:(j + 1) * LANES] = fq_cols[j] * (FOX_SCALE * LOG2E)
        return

    kv = jnp.dot(c_kv.astype(BF16), wkvb_ref[...], preferred_element_type=F32)
    for hh in range(N_HEADS):
        kb = kv[:, hh * LANES:(hh + 1) * LANES]
        ss = jnp.sum(kb * kb, axis=-1, keepdims=True)
        km_ref[hh] = (kb * lax.rsqrt(ss / NOPE_DIM + EPS) * gk_ref[...] + kr).astype(BF16)
    vm_ref[...] = kv[:, N_HEADS * LANES:].astype(BF16)
    vf_ref[...] = fv.astype(BF16)

    @pl.when(i == 0)
    def _():
        carry_ref[...] = jnp.zeros_like(carry_ref)

    lfm = jnp.where(lane < N_HEADS, lf, 0.0)
    parts = jnp.concatenate(_split3(lfm), axis=1).astype(BF16)
    r_i = lax.broadcasted_iota(jnp.int32, (tm, tm), 0)
    c_i = lax.broadcasted_iota(jnp.int32, (tm, tm), 1)
    tri = jnp.where(r_i >= c_i, 1.0, 0.0).astype(BF16)
    cs = jnp.dot(tri, parts, preferred_element_type=F32)
    c = cs[:, :LANES] + cs[:, LANES:2 * LANES] + cs[:, 2 * LANES:] + carry_ref[...]
    carry_ref[...] = c[tm - 1:tm, :]

    c_parts = _split3(c * LOG2E)
    for hh in range(N_HEADS):
        j, odd = hh // 2, hh % 2
        own = (lane >= HEAD_DIM) if odd else (lane < HEAD_DIM)
        l0 = 0 if odd else HEAD_DIM
        ch = [jnp.broadcast_to(p[:, hh:hh + 1], (tm, LANES)) for p in c_parts]
        qv = jnp.where(own, fq_cols[j] * (FOX_SCALE * LOG2E), 0.0)
        kvv = jnp.where(own, fk_cols[j], 0.0)
        for t in range(3):
            qv = jnp.where(lane == l0 + t, ch[t], qv)
            kvv = jnp.where(lane == l0 + 3 + t, -ch[t], kvv)
        qv = jnp.where((lane >= l0 + 3) & (lane < l0 + 6), 1.0, qv)
        kvv = jnp.where((lane >= l0) & (lane < l0 + 3), 1.0, kvv)
        qf_ref[hh] = qv.astype(BF16)
        kf_ref[hh] = kvv.astype(BF16)


def _project(x, w, *, prompt, pos_base, tm):
    b, s, d = x.shape
    grid = (b, s // tm)
    row = lambda n: pl.BlockSpec((None, tm, n), lambda bi, i: (bi, i, 0))
    head = pl.BlockSpec((None, N_HEADS, tm, LANES), lambda bi, i: (bi, 0, i, 0))
    full = lambda a: pl.BlockSpec(a.shape, lambda bi, i: (0,) * a.ndim)
    ins = [x, w["g_attn"], w["w_in"], w["g_qa"], w["w_qb"], w["g_kva"], w["w_kvb"], w["g_q"],
           w["g_k"], w["g_kr"], w["g_fq"], w["g_fk"], w["b_f"], w["inv_lane"]]
    in_specs = [row(d)] + [full(a) for a in ins[1:]]
    f32 = lambda n: jax.ShapeDtypeStruct((b, s, n), F32)
    out_shape = [f32(KV_LORA), f32(LANES), f32(N_HEADS * HEAD_DIM), f32(N_HEADS * HEAD_DIM), f32(LANES)]
    out_specs = [row(KV_LORA), row(LANES), row(N_HEADS * HEAD_DIM), row(N_HEADS * HEAD_DIM), row(LANES)]
    scratch = []
    if prompt:
        hb = jax.ShapeDtypeStruct((b, N_HEADS, s, LANES), BF16)
        vb = jax.ShapeDtypeStruct((b, s, N_HEADS * HEAD_DIM), BF16)
        out_shape += [hb, hb, vb, hb, hb, vb]
        out_specs += [head, head, row(N_HEADS * HEAD_DIM), head, head, row(N_HEADS * HEAD_DIM)]
        scratch = [pltpu.VMEM((1, LANES), F32)]
    else:
        out_shape += [jax.ShapeDtypeStruct((b, N_HEADS, s, LANES), F32), f32(N_HEADS * HEAD_DIM)]
        out_specs += [head, row(N_HEADS * HEAD_DIM)]
    return pl.pallas_call(
        functools.partial(_proj_kernel, tm=tm, pos_base=pos_base, prompt=prompt),
        grid=grid, in_specs=in_specs, out_specs=out_specs, out_shape=out_shape,
        scratch_shapes=scratch,
        compiler_params=pltpu.CompilerParams(
            dimension_semantics=("parallel", "arbitrary"), vmem_limit_bytes=VMEM_LIMIT),
        name="proj_prompt" if prompt else "proj_sample",
    )(*ins)


def _attn_kernel(qi_ref, ki_ref, q_ref, k_ref, v_ref, o_ref, m_ref, l_ref, acc_ref, *, tq):
    p = pl.program_id(2)
    qi, ki = qi_ref[p], ki_ref[p]
    n_col = tq // LANES

    @pl.when(ki == 0)
    def _():
        m_ref[...] = jnp.full_like(m_ref, NEG_INF)
        l_ref[...] = jnp.zeros_like(l_ref)
        acc_ref[...] = jnp.zeros_like(acc_ref)

    def step(diagonal):
        v = v_ref[...]
        if diagonal:
            r_i = lax.broadcasted_iota(jnp.int32, (tq, LANES), 0)
            c_i = lax.broadcasted_iota(jnp.int32, (tq, LANES), 1)
        for hh in range(2):
            s = lax.dot_general(q_ref[hh], k_ref[hh], _NT, preferred_element_type=F32)
            cols = [s[:, j * LANES:(j + 1) * LANES] for j in range(n_col)]
            if diagonal:
                cols = [jnp.where(r_i >= c_i + j * LANES, cj, NEG_INF) for j, cj in enumerate(cols)]
            m_prev = m_ref[hh]
            m_blk = functools.reduce(jnp.maximum, cols)
            m_new = jnp.maximum(m_prev, jnp.max(m_blk, axis=-1, keepdims=True))
            alpha = jnp.exp2(m_prev - m_new)
            pr = [jnp.exp2(cj - m_new) for cj in cols]
            l_blk = functools.reduce(jnp.add, pr)
            l_ref[hh] = alpha * l_ref[hh] + jnp.sum(l_blk, axis=-1, keepdims=True)
            pb = jnp.concatenate([x.astype(BF16) for x in pr], axis=1)
            acc_ref[hh] = alpha * acc_ref[hh] + jnp.dot(pb, v, preferred_element_type=F32)
            m_ref[hh] = m_new

    @pl.when(ki < qi)
    def _():
        step(False)

    @pl.when(ki == qi)
    def _():
        step(True)
        lane = _lane((tq, LANES))
        o_ref[...] = jnp.where(lane < HEAD_DIM, acc_ref[0] / l_ref[0], acc_ref[1] / l_ref[1])


def _prompt_attention(q, k, v, *, tq):
    b, _, s, _ = q.shape
    nq = s // tq
    pairs = [(a, c) for a in range(nq) for c in range(a + 1)]
    qi = jnp.asarray(np.array([a for a, _ in pairs], np.int32))
    ki = jnp.asarray(np.array([c for _, c in pairs], np.int32))
    grid_spec = pltpu.PrefetchScalarGridSpec(
        num_scalar_prefetch=2,
        grid=(b, N_HEADS // 2, len(pairs)),
        in_specs=[
            pl.BlockSpec((None, 2, tq, LANES), lambda bi, hp, p, qi, ki: (bi, hp, qi[p], 0)),
            pl.BlockSpec((None, 2, tq, LANES), lambda bi, hp, p, qi, ki: (bi, hp, ki[p], 0)),
            pl.BlockSpec((None, tq, LANES), lambda bi, hp, p, qi, ki: (bi, ki[p], hp)),
        ],
        out_specs=pl.BlockSpec((None, tq, LANES), lambda bi, hp, p, qi, ki: (bi, qi[p], hp)),
        scratch_shapes=[pltpu.VMEM((2, tq, LANES), F32), pltpu.VMEM((2, tq, LANES), F32),
                        pltpu.VMEM((2, tq, LANES), F32)],
    )
    return pl.pallas_call(
        functools.partial(_attn_kernel, tq=tq),
        grid_spec=grid_spec,
        out_shape=jax.ShapeDtypeStruct((b, s, N_HEADS * HEAD_DIM), F32),
        compiler_params=pltpu.CompilerParams(
            dimension_semantics=("parallel", "parallel", "arbitrary"), vmem_limit_bytes=VMEM_LIMIT),
        name="prompt_attention",
    )(qi, ki, q, k, v)


def _page_copies(pt_ref, seq, chunk, slot, n_pages, streams):
    out = []
    for j in range(n_pages):
        page = pt_ref[seq, chunk * n_pages + j]
        for hbm, buf, sem in streams:
            out.append(pltpu.make_async_copy(hbm.at[page], buf.at[slot, j], sem.at[slot]))
    return out


def _gather_step(pt_ref, n_chunks, n_pages, streams, chunk_of):
    b, c = pl.program_id(0), pl.program_id(1)
    step = b * n_chunks + c
    slot = step % 2
    total = pl.num_programs(0) * n_chunks

    @pl.when(step == 0)
    def _():
        for cp in _page_copies(pt_ref, b, chunk_of(c), slot, n_pages, streams):
            cp.start()

    @pl.when(step + 1 < total)
    def _():
        last = c == n_chunks - 1
        nb = jnp.where(last, b + 1, b)
        nc = jnp.where(last, 0, c + 1)
        for cp in _page_copies(pt_ref, nb, chunk_of(nc), 1 - slot, n_pages, streams):
            cp.start()

    for cp in _page_copies(pt_ref, b, chunk_of(c), slot, n_pages, streams):
        cp.wait()
    return slot


def _dec_mla_kernel(pt_ref, q_ref, cnew_ref, krnew_ref, wkabs_ref, wk_ref, wv_ref, gk_ref,
                    ckv_hbm, kr_hbm, o_ref, cbuf, kbuf, sem, m_ref, l_ref, acc_ref, qlat_ref,
                    *, n_chunks, n_pages, page):
    c = pl.program_id(1)
    streams = [(ckv_hbm, cbuf, sem.at[0]), (kr_hbm, kbuf, sem.at[1])]
    slot = _gather_step(pt_ref, n_chunks, n_pages, streams, lambda cc: cc)
    sub = lax.broadcasted_iota(jnp.int32, (N_HEADS, 1), 0)

    @pl.when(c == 0)
    def _():
        m_ref[...] = jnp.full_like(m_ref, NEG_INF)
        l_ref[...] = jnp.zeros_like(l_ref)
        acc_ref[...] = jnp.zeros_like(acc_ref)
        qg = (q_ref[...] * gk_ref[...]).astype(BF16)
        g_all = jnp.dot(qg, wkabs_ref[...], preferred_element_type=F32)
        q_lat = jnp.zeros((N_HEADS, KV_LORA), F32)
        for hh in range(N_HEADS):
            q_lat = q_lat + jnp.where(sub == hh, g_all[:, hh * KV_LORA:(hh + 1) * KV_LORA], 0.0)
        qlat_ref[...] = q_lat

    lane512 = _lane((N_HEADS, N_HEADS * NOPE_DIM))
    seg = jnp.where(lane512 // NOPE_DIM == sub, 1.0, 0.0).astype(BF16)
    q_lat = qlat_ref[...].astype(BF16)
    q_rope = q_ref[...][:, ROPE_OFF:ROPE_OFF + ROPE_DIM].astype(BF16)

    def nope_scores(c_rows):
        cb = c_rows.astype(BF16)
        kk = jnp.dot(cb, wk_ref[...], preferred_element_type=F32)
        ssq = lax.dot_general(seg, (kk * kk).astype(BF16), _NT, preferred_element_type=F32)
        s_n = lax.dot_general(q_lat, cb, _NT, preferred_element_type=F32)
        return s_n * lax.rsqrt(ssq / NOPE_DIM + EPS), cb

    def update(s, cb):
        m_prev = m_ref[...]
        m_new = jnp.maximum(m_prev, jnp.max(s, axis=-1, keepdims=True))
        alpha = jnp.exp2(m_prev - m_new)
        pr = jnp.exp2(s - m_new)
        l_ref[...] = alpha * l_ref[...] + jnp.sum(pr, axis=-1, keepdims=True)
        acc_ref[...] = alpha * acc_ref[...] + jnp.dot(pr.astype(BF16), cb, preferred_element_type=F32)
        m_ref[...] = m_new

    s_n, cb = nope_scores(cbuf[slot].reshape(n_pages * page, KV_LORA))
    s_r = jnp.concatenate(
        [jnp.dot(q_rope, kbuf[slot, j].astype(BF16), preferred_element_type=F32)
         for j in range(n_pages)], axis=1)
    update(s_n + s_r, cb)

    @pl.when(c == n_chunks - 1)
    def _():
        s_n, cb = nope_scores(cnew_ref[...])
        s_r = lax.dot_general(q_rope, krnew_ref[...].astype(BF16), _NT, preferred_element_type=F32)
        s = jnp.where(_lane(s_n.shape) == 0, s_n + s_r, NEG_INF)
        update(s, cb)
        o_lat = (acc_ref[...] / l_ref[...]).astype(BF16)
        o_all = jnp.dot(o_lat, wv_ref[...], preferred_element_type=F32)
        o_ref[...] = jnp.sum(jnp.where(lane512 // HEAD_DIM == sub, o_all, 0.0), axis=0, keepdims=True)


def _decode_mla(page_table, q, c_new, kr_new, w, cache_ckv, cache_kr_t, *, n_pages):
    b, total_pages = page_table.shape
    page = cache_ckv.shape[1]
    n_chunks = total_pages // n_pages
    full = lambda a: pl.BlockSpec(a.shape, lambda bi, ci, pt: (0,) * a.ndim)
    seq = lambda a: pl.BlockSpec((None,) + a.shape[1:], lambda bi, ci, pt: (bi,) + (0,) * (a.ndim - 1))
    consts = [w["wk_abs"], w["wk"], w["wv"], w["g_k"]]
    grid_spec = pltpu.PrefetchScalarGridSpec(
        num_scalar_prefetch=1,
        grid=(b, n_chunks),
        in_specs=[seq(q), seq(c_new), seq(kr_new)] + [full(a) for a in consts]
        + [pl.BlockSpec(memory_space=pl.ANY), pl.BlockSpec(memory_space=pl.ANY)],
        out_specs=pl.BlockSpec((None, 1, N_HEADS * HEAD_DIM), lambda bi, ci, pt: (bi, 0, 0)),
        scratch_shapes=[
            pltpu.VMEM((2, n_pages, page, KV_LORA), F32), pltpu.VMEM((2, n_pages, ROPE_DIM, page), F32),
            pltpu.SemaphoreType.DMA((2, 2)),
            pltpu.VMEM((N_HEADS, 1), F32), pltpu.VMEM((N_HEADS, 1), F32),
            pltpu.VMEM((N_HEADS, KV_LORA), F32), pltpu.VMEM((N_HEADS, KV_LORA), F32)],
    )
    return pl.pallas_call(
        functools.partial(_dec_mla_kernel, n_chunks=n_chunks, n_pages=n_pages, page=page),
        grid_spec=grid_spec,
        out_shape=jax.ShapeDtypeStruct((b, 1, N_HEADS * HEAD_DIM), F32),
        compiler_params=pltpu.CompilerParams(
            dimension_semantics=("arbitrary", "arbitrary"), vmem_limit_bytes=VMEM_LIMIT),
        name="decode_mla",
    )(page_table, q, c_new, kr_new, *consts, cache_ckv, cache_kr_t)


def _dot_f32(a, b, dims=None):
    if dims is None:
        return jnp.dot(a, b, precision=lax.Precision.HIGHEST, preferred_element_type=F32)
    return lax.dot_general(a, b, dims, precision=lax.Precision.HIGHEST, preferred_element_type=F32)


def _dec_fox_kernel(pt_ref, q_ref, qt_ref, knew_ref, vnew_ref, lfnew_ref, k_hbm, v_hbm, lf_hbm, o_ref,
                    kbuf, vbuf, lbuf, sem, qrep_ref, s_ref, m_ref, l_ref, acc_ref, accn_ref, carry_ref,
                    *, n_chunks, n_pages, page):
    c = pl.program_id(1)
    streams = [(k_hbm, kbuf, sem.at[0]), (v_hbm, vbuf, sem.at[1]), (lf_hbm, lbuf, sem.at[2])]
    slot = _gather_step(pt_ref, n_chunks, n_pages, streams, lambda cc: n_chunks - 1 - cc)

    @pl.when(c == 0)
    def _():
        m_ref[...] = jnp.sum(q_ref[...] * knew_ref[...], axis=-1, keepdims=True)
        l_ref[...] = jnp.ones_like(l_ref)
        acc_ref[...] = jnp.zeros_like(acc_ref)
        accn_ref[...] = vnew_ref[...]
        carry_ref[...] = lfnew_ref[...]
        for hh in range(N_HEADS):
            qrep_ref[hh] = jnp.broadcast_to(qt_ref[:, hh:hh + 1], (HEAD_DIM, page))

    def score_page(pg, carry):
        rows = [jnp.sum(kbuf[slot, pg, hh] * qrep_ref[hh], axis=0, keepdims=True)
                for hh in range(N_HEADS)]
        s_ref[pg] = jnp.concatenate(rows, axis=0)
        return carry

    lax.fori_loop(0, n_pages, score_page, 0)

    n_rows = n_pages * N_HEADS
    lf = lbuf[slot].reshape(n_rows, page)
    a_i = lax.broadcasted_iota(jnp.int32, (page, page), 0)
    b_i = lax.broadcasted_iota(jnp.int32, (page, page), 1)
    later_tok = jnp.where(a_i > b_i, 1.0, 0.0)
    r_i = lax.broadcasted_iota(jnp.int32, (n_rows, n_rows), 0)
    c_i = lax.broadcasted_iota(jnp.int32, (n_rows, n_rows), 1)
    later_page = jnp.where((c_i % N_HEADS == r_i % N_HEADS) & (c_i // N_HEADS > r_i // N_HEADS), 1.0, 0.0)
    row_tot = _dot_f32(lf, jnp.ones((page, page), F32))
    bias = _dot_f32(lf, later_tok) + _dot_f32(later_page, row_tot)
    carry = carry_ref[...]
    bias = (bias.reshape(n_pages, N_HEADS, page) + carry[None]) * LOG2E
    carry_ref[...] = carry + jnp.sum(row_tot.reshape(n_pages, N_HEADS, page), axis=0)

    lg = s_ref[...] + bias
    m_prev = m_ref[...]
    m_new = jnp.maximum(m_prev, jnp.max(jnp.max(lg, axis=0), axis=-1, keepdims=True))
    alpha = jnp.exp2(m_prev - m_new)
    pr = jnp.exp2(lg - m_new[None])
    s_ref[...] = pr
    l_ref[...] = alpha * l_ref[...] + jnp.sum(jnp.sum(pr, axis=0), axis=-1, keepdims=True)
    accn_ref[...] = alpha * accn_ref[...]
    m_ref[...] = m_new

    for hh in range(N_HEADS):
        def pv_page(pg, acc, hh=hh):
            return acc + s_ref[pg, pl.ds(hh, 1), :] * vbuf[slot, pg, hh]
        acc_ref[hh] = lax.fori_loop(0, n_pages, pv_page, alpha[hh:hh + 1, :] * acc_ref[hh])

    @pl.when(c == n_chunks - 1)
    def _():
        sub = lax.broadcasted_iota(jnp.int32, (N_HEADS, 1), 0)
        ones = jnp.ones((N_HEADS, page), F32)
        o = accn_ref[...]
        for hh in range(N_HEADS):
            o = o + jnp.where(sub == hh, _dot_f32(ones, acc_ref[hh], _NT), 0.0)
        o_ref[...] = o / l_ref[...]


def _decode_fox(page_table, q, q_t, k_new, v_new, lf_new, cache_k_t, cache_v_t, cache_lf_t, *, n_pages):
    b, total_pages = page_table.shape
    page = cache_k_t.shape[-1]
    n_chunks = total_pages // n_pages
    seq = lambda a: pl.BlockSpec((None,) + a.shape[1:], lambda bi, ci, pt: (bi,) + (0,) * (a.ndim - 1))
    hbm = pl.BlockSpec(memory_space=pl.ANY)
    grid_spec = pltpu.PrefetchScalarGridSpec(
        num_scalar_prefetch=1,
        grid=(b, n_chunks),
        in_specs=[seq(q), seq(q_t), seq(k_new), seq(v_new), seq(lf_new), hbm, hbm, hbm],
        out_specs=pl.BlockSpec((None, N_HEADS, HEAD_DIM), lambda bi, ci, pt: (bi, 0, 0)),
        scratch_shapes=[
            pltpu.VMEM((2, n_pages, N_HEADS, HEAD_DIM, page), F32),
            pltpu.VMEM((2, n_pages, N_HEADS, HEAD_DIM, page), F32),
            pltpu.VMEM((2, n_pages, N_HEADS, page), F32),
            pltpu.SemaphoreType.DMA((3, 2)),
            pltpu.VMEM((N_HEADS, HEAD_DIM, page), F32),
            pltpu.VMEM((n_pages, N_HEADS, page), F32),
            pltpu.VMEM((N_HEADS, 1), F32), pltpu.VMEM((N_HEADS, 1), F32),
            pltpu.VMEM((N_HEADS, HEAD_DIM, page), F32),
            pltpu.VMEM((N_HEADS, HEAD_DIM), F32),
            pltpu.VMEM((N_HEADS, page), F32)],
    )
    return pl.pallas_call(
        functools.partial(_dec_fox_kernel, n_chunks=n_chunks, n_pages=n_pages, page=page),
        grid_spec=grid_spec,
        out_shape=jax.ShapeDtypeStruct((b, N_HEADS, HEAD_DIM), F32),
        compiler_params=pltpu.CompilerParams(
            dimension_semantics=("arbitrary", "arbitrary"), vmem_limit_bytes=VMEM_LIMIT),
        name="decode_fox",
    )(page_table, q, q_t, k_new, v_new, lf_new, cache_k_t, cache_v_t, cache_lf_t)


def _post_kernel(x_ref, om_ref, of_ref, wo_ref, gffn_ref, wup_ref, wconv_ref, bconv_ref, wdown_ref,
                 *refs, tm, d_ff, bounds, prompt):
    if prompt:
        y_ref, tail_ref, ubuf = refs
    else:
        s0_ref, s1_ref, y_ref, u_ref = refs
    half = N_HEADS * HEAD_DIM
    x1 = (x_ref[...]
          + jnp.dot(om_ref[...].astype(BF16), wo_ref[:half, :], preferred_element_type=F32)
          + jnp.dot(of_ref[...].astype(BF16), wo_ref[half:, :], preferred_element_type=F32))
    h2 = _rms_rows(x1, gffn_ref[...]).astype(BF16)

    if prompt:
        @pl.when(pl.program_id(1) == 0)
        def _():
            ubuf[0:8, :] = jnp.zeros((8, 2 * d_ff), F32)

    acts = []
    for lo, hi in zip(bounds[:-1], bounds[1:]):
        hc = []
        for base in (0, d_ff):
            cols = slice(base + lo, base + hi)
            u = jnp.dot(h2, wup_ref[:, cols], preferred_element_type=F32)
            if prompt:
                ubuf[8:8 + tm, cols] = u
                u1, u2 = ubuf[7:7 + tm, cols], ubuf[6:6 + tm, cols]
                tail = ubuf[tm:tm + 8, cols]
                ubuf[0:8, cols] = tail
                tail_ref[:, cols] = tail
            else:
                u_ref[:, cols] = u
                u1, u2 = s1_ref[:, cols], s0_ref[:, cols]
            hc.append(bconv_ref[:, cols] + wconv_ref[0:1, cols] * u2 + wconv_ref[1:2, cols] * u1
                      + wconv_ref[2:3, cols] * u)
        gate, val = hc
        acts.append((gate * (1.0 / (1.0 + jnp.exp(-gate))) * val).astype(BF16))
    act = jnp.concatenate(acts, axis=1)
    y_ref[...] = x1 + jnp.dot(act, wdown_ref[...], preferred_element_type=F32)


def _ffn_bounds(d_ff):
    tiles = d_ff // MXU_K
    assert tiles * MXU_K == d_ff
    return (0, (tiles + 1) // 2 * MXU_K, d_ff)


def _post(x, o_mla, o_fox, w, state, *, prompt, tm):
    b, s, d = x.shape
    d_ff = w["w_down"].shape[0]
    grid = (b, s // tm)
    row = lambda n: pl.BlockSpec((None, tm, n), lambda bi, i: (bi, i, 0))
    full = lambda a: pl.BlockSpec(a.shape, lambda bi, i: (0,) * a.ndim)
    consts = [w["w_o"], w["g_ffn"], w["w_up"], w["w_conv"], w["b_conv"], w["w_down"]]
    ins = [x, o_mla, o_fox] + consts
    in_specs = [row(d), row(o_mla.shape[-1]), row(o_fox.shape[-1])] + [full(a) for a in consts]
    out_shape = [jax.ShapeDtypeStruct((b, s, d), F32)]
    out_specs = [row(d)]
    scratch = []
    if prompt:
        out_shape.append(jax.ShapeDtypeStruct((b, 8, 2 * d_ff), F32))
        out_specs.append(pl.BlockSpec((None, 8, 2 * d_ff), lambda bi, i: (bi, 0, 0)))
        scratch = [pltpu.VMEM((tm + 8, 2 * d_ff), F32)]
    else:
        ins += [state[0], state[1]]
        in_specs += [row(2 * d_ff), row(2 * d_ff)]
        out_shape.append(jax.ShapeDtypeStruct((b, s, 2 * d_ff), F32))
        out_specs.append(row(2 * d_ff))
    return pl.pallas_call(
        functools.partial(_post_kernel, tm=tm, d_ff=d_ff, bounds=_ffn_bounds(d_ff), prompt=prompt),
        grid=grid, in_specs=in_specs, out_specs=out_specs, out_shape=out_shape,
        scratch_shapes=scratch,
        compiler_params=pltpu.CompilerParams(
            dimension_semantics=("parallel", "arbitrary"), vmem_limit_bytes=VMEM_LIMIT),
        name="post_prompt" if prompt else "post_sample",
    )(*ins)


def _pad_cols(a, width, offset=0):
    return jnp.pad(a, ((0, 0), (offset, width - offset - a.shape[1])))


def _prepare_weights(g_attn, w_in, b_f, g_qa, w_qb, g_kva, w_kvb, g_qn, g_qr, g_kn, g_kr, g_fq,
                     g_fk, w_o, g_ffn, w_up, w_conv, b_conv, w_down):
    hd = N_HEADS * HEAD_DIM
    o_kva, o_kr = Q_LORA, Q_LORA + KV_LORA
    o_fq = o_kr + ROPE_DIM
    o_fk, o_fv, o_ff = o_fq + hd, o_fq + 2 * hd, o_fq + 3 * hd
    w_in_p = jnp.concatenate([
        w_in[:, :o_kr],
        _pad_cols(w_in[:, o_kr:o_fq], LANES, ROPE_OFF),
        w_in[:, o_fq:o_ff],
        _pad_cols(w_in[:, o_ff:], LANES)], axis=1).astype(BF16)
    d_q = NOPE_DIM + ROPE_DIM
    w_qb_p = jnp.pad(w_qb.reshape(Q_LORA, N_HEADS, d_q), ((0, 0), (0, 0), (0, LANES - d_q)))
    w_qb_p = w_qb_p.reshape(Q_LORA, N_HEADS * LANES).astype(BF16)
    w_kvb3 = w_kvb.reshape(KV_LORA, N_HEADS, NOPE_DIM + HEAD_DIM)
    wk3, wv3 = w_kvb3[:, :, :NOPE_DIM], w_kvb3[:, :, NOPE_DIM:]
    wk_pad = jnp.pad(wk3, ((0, 0), (0, 0), (0, LANES - NOPE_DIM))).reshape(KV_LORA, N_HEADS * LANES)
    wv = wv3.reshape(KV_LORA, hd)
    wk_abs = jnp.pad(jnp.transpose(wk3, (2, 1, 0)), ((0, LANES - NOPE_DIM), (0, 0), (0, 0)))
    half = ROPE_DIM // 2
    inv = ROPE_THETA ** (-jnp.arange(half, dtype=F32) / half)
    inv_lane = _pad_cols(jnp.tile(inv, 2)[None, :], LANES, ROPE_OFF)
    row = lambda a: a[None, :].astype(F32)
    return {
        "g_attn": row(g_attn), "w_in": w_in_p, "g_qa": row(g_qa), "w_qb": w_qb_p,
        "g_kva": row(g_kva), "w_kvb": jnp.concatenate([wk_pad, wv], axis=1).astype(BF16),
        "g_q": _pad_cols(jnp.concatenate([g_qn, g_qr])[None, :], LANES),
        "g_k": _pad_cols(row(g_kn), LANES), "g_kr": _pad_cols(row(g_kr), LANES, ROPE_OFF),
        "g_fq": row(jnp.tile(g_fq, 2)), "g_fk": row(jnp.tile(g_fk, 2)),
        "b_f": _pad_cols(row(b_f), LANES), "inv_lane": inv_lane,
        "wk_abs": wk_abs.reshape(LANES, N_HEADS * KV_LORA).astype(BF16),
        "wk": wk3.reshape(KV_LORA, hd).astype(BF16), "wv": wv.astype(BF16),
        "w_o": w_o.astype(BF16), "g_ffn": row(g_ffn), "w_up": w_up.astype(BF16),
        "w_conv": w_conv.astype(F32), "b_conv": row(b_conv), "w_down": w_down.astype(BF16),
    }


def _tile(n, pref):
    t = min(n, pref)
    assert n % t == 0, (n, pref)
    return t


def kernel(x_prompt, x_sample, cache_mla_ckv, cache_mla_krope, cache_fox_k, cache_fox_v, cache_fox_logf, state_ffn_conv, page_table, g_attn, w_in, b_f, g_qa, w_qb, g_kva, w_kvb, g_qn, g_qr, g_kn, g_kr, g_fq, g_fk, w_o, g_ffn, w_up, w_conv, b_conv, w_down):
    w = _prepare_weights(g_attn, w_in, b_f, g_qa, w_qb, g_kva, w_kvb, g_qn, g_qr, g_kn, g_kr,
                         g_fq, g_fk, w_o, g_ffn, w_up, w_conv, b_conv, w_down)
    bp, s, _ = x_prompt.shape
    bd, t_dec, d = x_sample.shape
    assert t_dec == 1
    n_pool, page = cache_mla_ckv.shape[:2]
    past = page_table.shape[1] * page
    hd = N_HEADS * HEAD_DIM

    (p_ckv, p_kr, p_fk, p_fv, p_lf, qm, km, vm, qf, kf, vf) = _project(
        x_prompt, w, prompt=True, pos_base=0, tm=_tile(s, 256))
    tq = _tile(s, 512)
    o_mla = _prompt_attention(qm, km, vm, tq=tq)
    o_fox = _prompt_attention(qf, kf, vf, tq=tq)
    y_p, tail = _post(x_prompt, o_mla, o_fox, w, None, prompt=True, tm=_tile(s, 256))

    xs = x_sample.reshape(1, bd, d)
    s_ckv, s_kr, s_fk, s_fv, s_lf, qs, qfs = _project(
        xs, w, prompt=False, pos_base=past, tm=_tile(bd, 128))
    s_kr32 = s_kr[0, :, ROPE_OFF:ROPE_OFF + ROPE_DIM]
    pad8 = lambda a: jnp.pad(a[:, None, :], ((0, 0), (0, 7), (0, 0)))
    n_dec_pages = _tile(page_table.shape[1], 16)
    o_mla_s = _decode_mla(
        page_table, jnp.transpose(qs[0], (1, 0, 2)), pad8(s_ckv[0]), pad8(s_kr32), w,
        cache_mla_ckv, jnp.transpose(cache_mla_krope, (0, 2, 1)), n_pages=n_dec_pages)
    qf3 = qfs[0].reshape(bd, N_HEADS, HEAD_DIM)
    lf_new = jnp.broadcast_to(s_lf[0, :, :N_HEADS, None], (bd, N_HEADS, page))
    o_fox_s = _decode_fox(
        page_table, qf3, jnp.transpose(qf3, (0, 2, 1)), s_fk[0].reshape(bd, N_HEADS, HEAD_DIM),
        s_fv[0].reshape(bd, N_HEADS, HEAD_DIM), lf_new,
        jnp.transpose(cache_fox_k, (0, 2, 3, 1)), jnp.transpose(cache_fox_v, (0, 2, 3, 1)),
        jnp.transpose(cache_fox_logf, (0, 2, 1)), n_pages=n_dec_pages)
    state = (state_ffn_conv[:, 0, :][None], state_ffn_conv[:, 1, :][None])
    y_s, u_s = _post(xs, o_mla_s.reshape(1, bd, hd), o_fox_s.reshape(1, bd, hd), w, state,
                     prompt=False, tm=_tile(bd, 128))

    return (
        y_p, y_s.reshape(bd, 1, d),
        p_ckv, p_kr[:, :, ROPE_OFF:ROPE_OFF + ROPE_DIM],
        p_fk.reshape(bp, s, N_HEADS, HEAD_DIM), p_fv.reshape(bp, s, N_HEADS, HEAD_DIM),
        p_lf[:, :, :N_HEADS], tail[:, 6:8, :],
        s_ckv.reshape(bd, 1, KV_LORA), s_kr32.reshape(bd, 1, ROPE_DIM),
        s_fk.reshape(bd, 1, N_HEADS, HEAD_DIM), s_fv.reshape(bd, 1, N_HEADS, HEAD_DIM),
        s_lf[0, :, :N_HEADS].reshape(bd, 1, N_HEADS),
        jnp.stack([state_ffn_conv[:, 1, :], u_s[0]], axis=1),
    )
```

```python
import functools

import jax
import jax.numpy as jnp
import numpy as np
from jax import lax
from jax.experimental import pallas as pl
from jax.experimental.pallas import tpu as pltpu

F32 = jnp.float32
BF16 = jnp.bfloat16

LANES = 128
MXU_K = 256
N_HEADS = 8
NOPE_DIM = 64
ROPE_DIM = 32
HEAD_DIM = 64
Q_LORA = 256
KV_LORA = 128
ROPE_THETA = 10000.0
EPS = 1e-6
NEG_INF = -1e30
LOG2E = 1.4426950408889634
MLA_SCALE = (NOPE_DIM + ROPE_DIM) ** -0.5
FOX_SCALE = HEAD_DIM ** -0.5
ROPE_OFF = 64
VMEM_LIMIT = 56 * 1024 * 1024
MLA_DEC_PAGES = 128
FOX_DEC_PAGES = 32
MLA_SUB_PAGES = 32
ATTN_HEADS = 8
ATTN_TQ = 1024

C_QA, C_KVA, C_KR, C_FQ, C_FK, C_FV, C_FF, C_END = 0, 256, 384, 512, 1024, 1536, 2048, 2176

_NT = (((1,), (1,)), ((), ()))


def _lane(shape):
    return lax.broadcasted_iota(jnp.int32, shape, len(shape) - 1)


def _rms_rows(x, g):
    return x * lax.rsqrt(jnp.mean(x * x, axis=-1, keepdims=True) + EPS) * g


def _split3(x):
    hi = x.astype(BF16).astype(F32)
    mid = (x - hi).astype(BF16).astype(F32)
    lo = x - hi - mid
    return hi, mid, lo


def _rope_tables(pos, inv_lane, lane):
    ang = pos * inv_lane
    cos, sin = jnp.cos(ang), jnp.sin(ang)
    half = ROPE_DIM // 2
    is_rope = (lane >= ROPE_OFF) & (lane < ROPE_OFF + ROPE_DIM)
    first = (lane >= ROPE_OFF) & (lane < ROPE_OFF + half)
    second = (lane >= ROPE_OFF + half) & (lane < ROPE_OFF + ROPE_DIM)
    return jnp.where(is_rope, cos, 1.0), jnp.where(second, sin, 0.0), jnp.where(first, -sin, 0.0)


def _apply_rope(x, tables):
    c, s_from_first, s_from_second = tables
    half = ROPE_DIM // 2
    return (x * c + pltpu.roll(x, half, 1) * s_from_first
            + pltpu.roll(x, LANES - half, 1) * s_from_second)


def _pair_norm(x, g_pair, lane):
    lo = lane < HEAD_DIM
    x2 = x * x
    ss_lo = jnp.sum(jnp.where(lo, x2, 0.0), axis=-1, keepdims=True)
    ss_hi = jnp.sum(jnp.where(lo, 0.0, x2), axis=-1, keepdims=True)
    r = jnp.where(lo, lax.rsqrt(ss_lo / HEAD_DIM + EPS), lax.rsqrt(ss_hi / HEAD_DIM + EPS))
    return x * r * g_pair


def _log_sigmoid(x):
    return jnp.minimum(x, 0.0) - jnp.log1p(jnp.exp(-jnp.abs(x)))


def _proj_kernel(x_ref, gattn_ref, win_ref, gqa_ref, wqb_ref, gkva_ref, wkvb_ref, gq_ref, gk_ref,
                 gkr_ref, gfq_ref, gfk_ref, bf_ref, inv_ref, *refs, tm, pos_base, prompt):
    if prompt:
        (ckv_ref, kr_ref, fk_ref, fv_ref, lf_ref, qm_ref, km_ref, vm_ref, qf_ref, kf_ref, vf_ref,
         carry_ref) = refs
    else:
        ckv_ref, kr_ref, fk_ref, fv_ref, lf_ref, qm_ref, qf_ref = refs
    i = pl.program_id(1)
    lane = _lane((tm, LANES))

    x = x_ref[...]
    h = _rms_rows(x, gattn_ref[...])
    z = jnp.dot(h.astype(BF16), win_ref[...], preferred_element_type=F32)

    if prompt:
        pos = (pos_base + i * tm + lax.broadcasted_iota(jnp.int32, (tm, 1), 0)).astype(F32)
    else:
        pos = jnp.full((tm, 1), pos_base, F32)
    tables = _rope_tables(pos, inv_ref[...], lane)

    c_kv = _rms_rows(z[:, C_KVA:C_KR], gkva_ref[...])
    ckv_ref[...] = c_kv
    krz = z[:, C_KR:C_FQ]
    kr_ss = jnp.sum(krz * krz, axis=-1, keepdims=True)
    kr = _apply_rope(krz * lax.rsqrt(kr_ss / ROPE_DIM + EPS) * gkr_ref[...], tables)
    kr_ref[...] = kr

    q_a = _rms_rows(z[:, C_QA:C_KVA], gqa_ref[...])
    q = jnp.dot(q_a.astype(BF16), wqb_ref[...], preferred_element_type=F32)
    is_nope = lane < NOPE_DIM
    is_rope = (lane >= ROPE_OFF) & (lane < ROPE_OFF + ROPE_DIM)
    for hh in range(N_HEADS):
        qb = q[:, hh * LANES:(hh + 1) * LANES]
        q2 = qb * qb
        ss_n = jnp.sum(jnp.where(is_nope, q2, 0.0), axis=-1, keepdims=True)
        ss_r = jnp.sum(jnp.where(is_rope, q2, 0.0), axis=-1, keepdims=True)
        r = jnp.where(is_nope, lax.rsqrt(ss_n / NOPE_DIM + EPS), lax.rsqrt(ss_r / ROPE_DIM + EPS))
        qn = _apply_rope(qb * r * gq_ref[...], tables)
        qm_ref[hh] = (qn * (MLA_SCALE * LOG2E)).astype(qm_ref.dtype)

    fq_cols, fk_cols = [], []
    for j in range(N_HEADS // 2):
        sl = slice(j * LANES, (j + 1) * LANES)
        fq_cols.append(_pair_norm(z[:, C_FQ:C_FK][:, sl], gfq_ref[...], lane))
        fkn = _pair_norm(z[:, C_FK:C_FV][:, sl], gfk_ref[...], lane)
        fk_cols.append(fkn)
        fk_ref[:, sl] = fkn
    fv = z[:, C_FV:C_FF]
    fv_ref[...] = fv
    lf = _log_sigmoid(z[:, C_FF:C_END] + bf_ref[...])
    lf_ref[...] = lf

    if not prompt:
        for j in range(N_HEADS // 2):
            qf_ref[:, j * LANES:(j + 1) * LANES] = fq_cols[j] * (FOX_SCALE * LOG2E)
        return

    kv = jnp.dot(c_kv.astype(BF16), wkvb_ref[...], preferred_element_type=F32)
    for hh in range(N_HEADS):
        kb = kv[:, hh * LANES:(hh + 1) * LANES]
        ss = jnp.sum(kb * kb, axis=-1, keepdims=True)
        km_ref[hh] = (kb * lax.rsqrt(ss / NOPE_DIM + EPS) * gk_ref[...] + kr).astype(BF16)
    vm_ref[...] = kv[:, N_HEADS * LANES:].astype(BF16)
    vf_ref[...] = fv.astype(BF16)

    @pl.when(i == 0)
    def _():
        carry_ref[...] = jnp.zeros_like(carry_ref)

    lfm = jnp.where(lane < N_HEADS, lf, 0.0)
    parts = jnp.concatenate(_split3(lfm), axis=1).astype(BF16)
    r_i = lax.broadcasted_iota(jnp.int32, (tm, tm), 0)
    c_i = lax.broadcasted_iota(jnp.int32, (tm, tm), 1)
    tri = jnp.where(r_i >= c_i, 1.0, 0.0).astype(BF16)
    cs = jnp.dot(tri, parts, preferred_element_type=F32)
    c = cs[:, :LANES] + cs[:, LANES:2 * LANES] + cs[:, 2 * LANES:] + carry_ref[...]
    carry_ref[...] = c[tm - 1:tm, :]

    c_parts = _split3(c * LOG2E)
    for hh in range(N_HEADS):
        j, odd = hh // 2, hh % 2
        own = (lane >= HEAD_DIM) if odd else (lane < HEAD_DIM)
        l0 = 0 if odd else HEAD_DIM
        ch = [jnp.broadcast_to(p[:, hh:hh + 1], (tm, LANES)) for p in c_parts]
        qv = jnp.where(own, fq_cols[j] * (FOX_SCALE * LOG2E), 0.0)
        kvv = jnp.where(own, fk_cols[j], 0.0)
        for t in range(3):
            qv = jnp.where(lane == l0 + t, ch[t], qv)
            kvv = jnp.where(lane == l0 + 3 + t, -ch[t], kvv)
        qv = jnp.where((lane >= l0 + 3) & (lane < l0 + 6), 1.0, qv)
        kvv = jnp.where((lane >= l0) & (lane < l0 + 3), 1.0, kvv)
        qf_ref[hh] = qv.astype(BF16)
        kf_ref[hh] = kvv.astype(BF16)


def _project(x, w, *, prompt, pos_base, tm):
    b, s, d = x.shape
    grid = (b, s // tm)
    row = lambda n: pl.BlockSpec((None, tm, n), lambda bi, i: (bi, i, 0))
    head = pl.BlockSpec((None, N_HEADS, tm, LANES), lambda bi, i: (bi, 0, i, 0))
    full = lambda a: pl.BlockSpec(a.shape, lambda bi, i: (0,) * a.ndim)
    ins = [x, w["g_attn"], w["w_in"], w["g_qa"], w["w_qb"], w["g_kva"], w["w_kvb"], w["g_q"],
           w["g_k"], w["g_kr"], w["g_fq"], w["g_fk"], w["b_f"], w["inv_lane"]]
    in_specs = [row(d)] + [full(a) for a in ins[1:]]
    f32 = lambda n: jax.ShapeDtypeStruct((b, s, n), F32)
    out_shape = [f32(KV_LORA), f32(LANES), f32(N_HEADS * HEAD_DIM), f32(N_HEADS * HEAD_DIM), f32(LANES)]
    out_specs = [row(KV_LORA), row(LANES), row(N_HEADS * HEAD_DIM), row(N_HEADS * HEAD_DIM), row(LANES)]
    scratch = []
    if prompt:
        hb = jax.ShapeDtypeStruct((b, N_HEADS, s, LANES), BF16)
        vb = jax.ShapeDtypeStruct((b, s, N_HEADS * HEAD_DIM), BF16)
        out_shape += [hb, hb, vb, hb, hb, vb]
        out_specs += [head, head, row(N_HEADS * HEAD_DIM), head, head, row(N_HEADS * HEAD_DIM)]
        scratch = [pltpu.VMEM((1, LANES), F32)]
    else:
        out_shape += [jax.ShapeDtypeStruct((b, N_HEADS, s, LANES), F32), f32(N_HEADS * HEAD_DIM)]
        out_specs += [head, row(N_HEADS * HEAD_DIM)]
    return pl.pallas_call(
        functools.partial(_proj_kernel, tm=tm, pos_base=pos_base, prompt=prompt),
        grid=grid, in_specs=in_specs, out_specs=out_specs, out_shape=out_shape,
        scratch_shapes=scratch,
        compiler_params=pltpu.CompilerParams(
            dimension_semantics=("parallel", "arbitrary"), vmem_limit_bytes=VMEM_LIMIT),
        name="proj_prompt" if prompt else "proj_sample",
    )(*ins)


def _attn_kernel(qi_ref, ki_ref, q_ref, k_ref, v_ref, o_ref, m_ref, l_ref, acc_ref, *, tq):
    p = pl.program_id(2)
    qi, ki = qi_ref[p], ki_ref[p]
    n_col = tq // LANES

    @pl.when(ki == 0)
    def _():
        m_ref[...] = jnp.full_like(m_ref, NEG_INF)
        l_ref[...] = jnp.zeros_like(l_ref)
        acc_ref[...] = jnp.zeros_like(acc_ref)

    def step(diagonal):
        if diagonal:
            r_i = lax.broadcasted_iota(jnp.int32, (tq, LANES), 0)
            c_i = lax.broadcasted_iota(jnp.int32, (tq, LANES), 1)
        for hh in range(ATTN_HEADS):
            v = v_ref[:, (hh // 2) * LANES:(hh // 2 + 1) * LANES]
            s = lax.dot_general(q_ref[hh], k_ref[hh], _NT, preferred_element_type=F32)
            cols = [s[:, j * LANES:(j + 1) * LANES] for j in range(n_col)]
            if diagonal:
                cols = [jnp.where(r_i >= c_i + j * LANES, cj, NEG_INF) for j, cj in enumerate(cols)]
            m_prev = m_ref[hh]
            m_blk = functools.reduce(jnp.maximum, cols)
            m_new = jnp.maximum(m_prev, jnp.max(m_blk, axis=-1, keepdims=True))
            alpha = jnp.exp2(m_prev - m_new)
            pr = [jnp.exp2(cj - m_new) for cj in cols]
            l_blk = functools.reduce(jnp.add, pr)
            l_ref[hh] = alpha * l_ref[hh] + jnp.sum(l_blk, axis=-1, keepdims=True)
            pb = jnp.concatenate([x.astype(BF16) for x in pr], axis=1)
            acc_ref[hh] = alpha * acc_ref[hh] + jnp.dot(pb, v, preferred_element_type=F32)
            m_ref[hh] = m_new

    @pl.when(ki < qi)
    def _():
        step(False)

    @pl.when(ki == qi)
    def _():
        step(True)
        lane = _lane((tq, LANES))
        for j in range(ATTN_HEADS // 2):
            o_ref[:, j * LANES:(j + 1) * LANES] = jnp.where(
                lane < HEAD_DIM, acc_ref[2 * j] / l_ref[2 * j], acc_ref[2 * j + 1] / l_ref[2 * j + 1])


def _prompt_attention(q, k, v, *, tq):
    b, _, s, _ = q.shape
    nq = s // tq
    pairs = [(a, c) for a in range(nq) for c in range(a + 1)]
    qi = jnp.asarray(np.array([a for a, _ in pairs], np.int32))
    ki = jnp.asarray(np.array([c for _, c in pairs], np.int32))
    width = ATTN_HEADS * HEAD_DIM
    grid_spec = pltpu.PrefetchScalarGridSpec(
        num_scalar_prefetch=2,
        grid=(b, N_HEADS // ATTN_HEADS, len(pairs)),
        in_specs=[
            pl.BlockSpec((None, ATTN_HEADS, tq, LANES), lambda bi, hp, p, qi, ki: (bi, hp, qi[p], 0)),
            pl.BlockSpec((None, ATTN_HEADS, tq, LANES), lambda bi, hp, p, qi, ki: (bi, hp, ki[p], 0)),
            pl.BlockSpec((None, tq, width), lambda bi, hp, p, qi, ki: (bi, ki[p], hp)),
        ],
        out_specs=pl.BlockSpec((None, tq, width), lambda bi, hp, p, qi, ki: (bi, qi[p], hp)),
        scratch_shapes=[pltpu.VMEM((ATTN_HEADS, tq, LANES), F32), pltpu.VMEM((ATTN_HEADS, tq, LANES), F32),
                        pltpu.VMEM((ATTN_HEADS, tq, LANES), F32)],
    )
    return pl.pallas_call(
        functools.partial(_attn_kernel, tq=tq),
        grid_spec=grid_spec,
        out_shape=jax.ShapeDtypeStruct((b, s, N_HEADS * HEAD_DIM), F32),
        compiler_params=pltpu.CompilerParams(
            dimension_semantics=("parallel", "parallel", "arbitrary"), vmem_limit_bytes=VMEM_LIMIT),
        name="prompt_attention",
    )(qi, ki, q, k, v)


def _page_copies(pt_ref, seq, chunk, slot, n_pages, streams):
    out = []
    for j in range(n_pages):
        page = pt_ref[seq, chunk * n_pages + j]
        for hbm, buf, sem in streams:
            out.append(pltpu.make_async_copy(hbm.at[page], buf.at[slot, j], sem.at[slot]))
    return out


def _gather_step(pt_ref, n_chunks, n_pages, streams, chunk_of):
    b, c = pl.program_id(0), pl.program_id(1)
    step = b * n_chunks + c
    slot = step % 2
    total = pl.num_programs(0) * n_chunks

    @pl.when(step == 0)
    def _():
        for cp in _page_copies(pt_ref, b, chunk_of(c), slot, n_pages, streams):
            cp.start()

    @pl.when(step + 1 < total)
    def _():
        last = c == n_chunks - 1
        nb = jnp.where(last, b + 1, b)
        nc = jnp.where(last, 0, c + 1)
        for cp in _page_copies(pt_ref, nb, chunk_of(nc), 1 - slot, n_pages, streams):
            cp.start()

    for cp in _page_copies(pt_ref, b, chunk_of(c), slot, n_pages, streams):
        cp.wait()
    return slot


def _dec_mla_kernel(pt_ref, q_ref, cnew_ref, krnew_ref, wkabs_ref, wkt_ref, wv_ref, gk_ref,
                    ckv_hbm, kr_hbm, o_ref, cbuf, kbuf, sem, m_ref, l_ref, acc_ref, w1_ref,
                    *, n_chunks, n_pages, page):
    c = pl.program_id(1)
    streams = [(ckv_hbm, cbuf, sem.at[0]), (kr_hbm, kbuf, sem.at[1])]
    slot = _gather_step(pt_ref, n_chunks, n_pages, streams, lambda cc: cc)
    sub = lax.broadcasted_iota(jnp.int32, (N_HEADS, 1), 0)

    @pl.when(c == 0)
    def _():
        m_ref[...] = jnp.full_like(m_ref, NEG_INF)
        l_ref[...] = jnp.zeros_like(l_ref)
        acc_ref[...] = jnp.zeros_like(acc_ref)
        qg = (q_ref[...] * gk_ref[...]).astype(BF16)
        g_all = jnp.dot(qg, wkabs_ref[...], preferred_element_type=F32)
        q_lat = jnp.zeros((N_HEADS, KV_LORA), F32)
        for hh in range(N_HEADS):
            q_lat = q_lat + jnp.where(sub == hh, g_all[:, hh * KV_LORA:(hh + 1) * KV_LORA], 0.0)
        w1_ref[0:N_HEADS * NOPE_DIM, :] = wkt_ref[...]
        w1_ref[N_HEADS * NOPE_DIM:, :] = jnp.concatenate(
            [q_lat, jnp.zeros_like(q_lat)], axis=0).astype(BF16)

    lane512 = _lane((N_HEADS, N_HEADS * NOPE_DIM))
    q_rope = q_ref[...][:, ROPE_OFF:ROPE_OFF + ROPE_DIM].astype(BF16)

    def nope_scores(c_rows):
        cb = c_rows.astype(BF16)
        both = lax.dot_general(w1_ref[...], cb, _NT, preferred_element_type=F32)
        kk = both[:N_HEADS * NOPE_DIM].reshape(N_HEADS, NOPE_DIM, cb.shape[0])
        ssq = jnp.sum(kk * kk, axis=1)
        s_n = both[N_HEADS * NOPE_DIM:N_HEADS * NOPE_DIM + N_HEADS]
        return s_n * lax.rsqrt(ssq / NOPE_DIM + EPS), cb

    def update(s, cb):
        m_prev = m_ref[...]
        m_new = jnp.maximum(m_prev, jnp.max(s, axis=-1, keepdims=True))
        alpha = jnp.exp2(m_prev - m_new)
        pr = jnp.exp2(s - m_new)
        l_ref[...] = alpha * l_ref[...] + jnp.sum(pr, axis=-1, keepdims=True)
        acc_ref[...] = alpha * acc_ref[...] + jnp.dot(pr.astype(BF16), cb, preferred_element_type=F32)
        m_ref[...] = m_new

    sub_pages = min(n_pages, MLA_SUB_PAGES)
    s_parts, cb_parts = [], []
    for j0 in range(0, n_pages, sub_pages):
        s_n, cb = nope_scores(cbuf[slot, j0:j0 + sub_pages].reshape(sub_pages * page, KV_LORA))
        kr = jnp.concatenate([kbuf[slot, j] for j in range(j0, j0 + sub_pages)], axis=1)
        s_parts.append(s_n + jnp.dot(q_rope, kr.astype(BF16), preferred_element_type=F32))
        cb_parts.append(cb)
    update(jnp.concatenate(s_parts, axis=1), jnp.concatenate(cb_parts, axis=0))

    @pl.when(c == n_chunks - 1)
    def _():
        s_n, cb = nope_scores(cnew_ref[...])
        s_r = lax.dot_general(q_rope, krnew_ref[...].astype(BF16), _NT, preferred_element_type=F32)
        s = jnp.where(_lane(s_n.shape) == 0, s_n + s_r, NEG_INF)
        update(s, cb)
        o_lat = (acc_ref[...] / l_ref[...]).astype(BF16)
        o_all = jnp.dot(o_lat, wv_ref[...], preferred_element_type=F32)
        o_ref[...] = jnp.sum(jnp.where(lane512 // HEAD_DIM == sub, o_all, 0.0), axis=0, keepdims=True)


def _decode_mla(page_table, q, c_new, kr_new, w, cache_ckv, cache_kr_t, *, n_pages):
    b, total_pages = page_table.shape
    page = cache_ckv.shape[1]
    n_chunks = total_pages // n_pages
    full = lambda a: pl.BlockSpec(a.shape, lambda bi, ci, pt: (0,) * a.ndim)
    seq = lambda a: pl.BlockSpec((None,) + a.shape[1:], lambda bi, ci, pt: (bi,) + (0,) * (a.ndim - 1))
    consts = [w["wk_abs"], w["wk_t"], w["wv"], w["g_k"]]
    grid_spec = pltpu.PrefetchScalarGridSpec(
        num_scalar_prefetch=1,
        grid=(b, n_chunks),
        in_specs=[seq(q), seq(c_new), seq(kr_new)] + [full(a) for a in consts]
        + [pl.BlockSpec(memory_space=pl.ANY), pl.BlockSpec(memory_space=pl.ANY)],
        out_specs=pl.BlockSpec((None, 1, N_HEADS * HEAD_DIM), lambda bi, ci, pt: (bi, 0, 0)),
        scratch_shapes=[
            pltpu.VMEM((2, n_pages, page, KV_LORA), F32), pltpu.VMEM((2, n_pages, ROPE_DIM, page), F32),
            pltpu.SemaphoreType.DMA((2, 2)),
            pltpu.VMEM((N_HEADS, 1), F32), pltpu.VMEM((N_HEADS, 1), F32),
            pltpu.VMEM((N_HEADS, KV_LORA), F32),
            pltpu.VMEM((N_HEADS * NOPE_DIM + 2 * N_HEADS, KV_LORA), BF16)],
    )
    return pl.pallas_call(
        functools.partial(_dec_mla_kernel, n_chunks=n_chunks, n_pages=n_pages, page=page),
        grid_spec=grid_spec,
        out_shape=jax.ShapeDtypeStruct((b, 1, N_HEADS * HEAD_DIM), F32),
        compiler_params=pltpu.CompilerParams(
            dimension_semantics=("arbitrary", "arbitrary"), vmem_limit_bytes=VMEM_LIMIT),
        name="decode_mla",
    )(page_table, q, c_new, kr_new, *consts, cache_ckv, cache_kr_t)


def _dot_f32(a, b, dims=None):
    if dims is None:
        return jnp.dot(a, b, precision=lax.Precision.HIGHEST, preferred_element_type=F32)
    return lax.dot_general(a, b, dims, precision=lax.Precision.HIGHEST, preferred_element_type=F32)


def _dec_fox_kernel(pt_ref, q_ref, qt_ref, knew_ref, vnew_ref, lfnew_ref, k_hbm, v_hbm, lf_hbm, o_ref,
                    kbuf, vbuf, lbuf, sem, qrep_ref, s_ref, m_ref, l_ref, acc_ref, accn_ref, carry_ref,
                    *, n_chunks, n_pages, page):
    c = pl.program_id(1)
    streams = [(k_hbm, kbuf, sem.at[0]), (v_hbm, vbuf, sem.at[1]), (lf_hbm, lbuf, sem.at[2])]
    slot = _gather_step(pt_ref, n_chunks, n_pages, streams, lambda cc: n_chunks - 1 - cc)

    @pl.when(c == 0)
    def _():
        m_ref[...] = jnp.sum(q_ref[...] * knew_ref[...], axis=-1, keepdims=True)
        l_ref[...] = jnp.ones_like(l_ref)
        acc_ref[...] = jnp.zeros_like(acc_ref)
        accn_ref[...] = vnew_ref[...]
        carry_ref[...] = lfnew_ref[...]
        for hh in range(N_HEADS):
            qrep_ref[hh] = jnp.broadcast_to(qt_ref[:, hh:hh + 1], (HEAD_DIM, page))

    def score_page(pg, carry):
        rows = [jnp.sum(kbuf[slot, pg, hh] * qrep_ref[hh], axis=0, keepdims=True)
                for hh in range(N_HEADS)]
        s_ref[pg] = jnp.concatenate(rows, axis=0)
        return carry

    lax.fori_loop(0, n_pages, score_page, 0)

    lf = lbuf[slot]
    a_i = lax.broadcasted_iota(jnp.int32, (page, page), 0)
    b_i = lax.broadcasted_iota(jnp.int32, (page, page), 1)
    later_tok = jnp.where(a_i > b_i, 1.0, 0.0)
    in_page = _dot_f32(lf.reshape(n_pages * N_HEADS, page), later_tok).reshape(n_pages, N_HEADS, page)
    tot = jnp.sum(lf, axis=-1, keepdims=True)
    running = carry_ref[...]
    rows = [None] * n_pages
    for pg in reversed(range(n_pages)):
        rows[pg] = in_page[pg] + running
        running = running + tot[pg]
    carry_ref[...] = running
    bias = jnp.stack(rows) * LOG2E

    lg = s_ref[...] + bias
    m_prev = m_ref[...]
    m_new = jnp.maximum(m_prev, jnp.max(jnp.max(lg, axis=0), axis=-1, keepdims=True))
    alpha = jnp.exp2(m_prev - m_new)
    pr = jnp.exp2(lg - m_new[None])
    s_ref[...] = pr
    l_ref[...] = alpha * l_ref[...] + jnp.sum(jnp.sum(pr, axis=0), axis=-1, keepdims=True)
    accn_ref[...] = alpha * accn_ref[...]
    m_ref[...] = m_new

    for hh in range(N_HEADS):
        def pv_page(pg, acc, hh=hh):
            return acc + s_ref[pg, pl.ds(hh, 1), :] * vbuf[slot, pg, hh]
        acc_ref[hh] = lax.fori_loop(0, n_pages, pv_page, alpha[hh:hh + 1, :] * acc_ref[hh])

    @pl.when(c == n_chunks - 1)
    def _():
        sub = lax.broadcasted_iota(jnp.int32, (N_HEADS, 1), 0)
        ones = jnp.ones((N_HEADS, page), F32)
        o = accn_ref[...]
        for hh in range(N_HEADS):
            o = o + jnp.where(sub == hh, _dot_f32(ones, acc_ref[hh], _NT), 0.0)
        o_ref[...] = o / l_ref[...]


def _decode_fox(page_table, q, q_t, k_new, v_new, lf_new, cache_k_t, cache_v_t, cache_lf_t, *, n_pages):
    b, total_pages = page_table.shape
    page = cache_k_t.shape[-1]
    n_chunks = total_pages // n_pages
    seq = lambda a: pl.BlockSpec((None,) + a.shape[1:], lambda bi, ci, pt: (bi,) + (0,) * (a.ndim - 1))
    hbm = pl.BlockSpec(memory_space=pl.ANY)
    grid_spec = pltpu.PrefetchScalarGridSpec(
        num_scalar_prefetch=1,
        grid=(b, n_chunks),
        in_specs=[seq(q), seq(q_t), seq(k_new), seq(v_new), seq(lf_new), hbm, hbm, hbm],
        out_specs=pl.BlockSpec((None, N_HEADS, HEAD_DIM), lambda bi, ci, pt: (bi, 0, 0)),
        scratch_shapes=[
            pltpu.VMEM((2, n_pages, N_HEADS, HEAD_DIM, page), F32),
            pltpu.VMEM((2, n_pages, N_HEADS, HEAD_DIM, page), F32),
            pltpu.VMEM((2, n_pages, N_HEADS, page), F32),
            pltpu.SemaphoreType.DMA((3, 2)),
            pltpu.VMEM((N_HEADS, HEAD_DIM, page), F32),
            pltpu.VMEM((n_pages, N_HEADS, page), F32),
            pltpu.VMEM((N_HEADS, 1), F32), pltpu.VMEM((N_HEADS, 1), F32),
            pltpu.VMEM((N_HEADS, HEAD_DIM, page), F32),
            pltpu.VMEM((N_HEADS, HEAD_DIM), F32),
            pltpu.VMEM((N_HEADS, page), F32)],
    )
    return pl.pallas_call(
        functools.partial(_dec_fox_kernel, n_chunks=n_chunks, n_pages=n_pages, page=page),
        grid_spec=grid_spec,
        out_shape=jax.ShapeDtypeStruct((b, N_HEADS, HEAD_DIM), F32),
        compiler_params=pltpu.CompilerParams(
            dimension_semantics=("arbitrary", "arbitrary"), vmem_limit_bytes=VMEM_LIMIT),
        name="decode_fox",
    )(page_table, q, q_t, k_new, v_new, lf_new, cache_k_t, cache_v_t, cache_lf_t)


def _post_kernel(x_ref, om_ref, of_ref, wo_ref, gffn_ref, wup_ref, wconv_ref, bconv_ref, wdown_ref,
                 *refs, tm, d_ff, bounds, prompt):
    if prompt:
        y_ref, tail_ref, ubuf = refs
    else:
        s0_ref, s1_ref, y_ref, u_ref = refs
    half = N_HEADS * HEAD_DIM
    x1 = (x_ref[...]
          + jnp.dot(om_ref[...].astype(BF16), wo_ref[:half, :], preferred_element_type=F32)
          + jnp.dot(of_ref[...].astype(BF16), wo_ref[half:, :], preferred_element_type=F32))
    h2 = _rms_rows(x1, gffn_ref[...]).astype(BF16)

    if prompt:
        @pl.when(pl.program_id(1) == 0)
        def _():
            ubuf[0:8, :] = jnp.zeros((8, 2 * d_ff), F32)

    acts = []
    for lo, hi in zip(bounds[:-1], bounds[1:]):
        hc = []
        for base in (0, d_ff):
            cols = slice(base + lo, base + hi)
            u = jnp.dot(h2, wup_ref[:, cols], preferred_element_type=F32)
            if prompt:
                ubuf[8:8 + tm, cols] = u
                u1, u2 = ubuf[7:7 + tm, cols], ubuf[6:6 + tm, cols]
                tail = ubuf[tm:tm + 8, cols]
                ubuf[0:8, cols] = tail
                tail_ref[:, cols] = tail
            else:
                u_ref[:, cols] = u
                u1, u2 = s1_ref[:, cols], s0_ref[:, cols]
            hc.append(bconv_ref[:, cols] + wconv_ref[0:1, cols] * u2 + wconv_ref[1:2, cols] * u1
                      + wconv_ref[2:3, cols] * u)
        gate, val = hc
        acts.append((gate * (1.0 / (1.0 + jnp.exp(-gate))) * val).astype(BF16))
    act = jnp.concatenate(acts, axis=1)
    y_ref[...] = x1 + jnp.dot(act, wdown_ref[...], preferred_element_type=F32)


def _ffn_bounds(d_ff):
    tiles = d_ff // MXU_K
    assert tiles * MXU_K == d_ff
    return (0, (tiles + 1) // 2 * MXU_K, d_ff)


def _post(x, o_mla, o_fox, w, state, *, prompt, tm):
    b, s, d = x.shape
    d_ff = w["w_down"].shape[0]
    grid = (b, s // tm)
    row = lambda n: pl.BlockSpec((None, tm, n), lambda bi, i: (bi, i, 0))
    full = lambda a: pl.BlockSpec(a.shape, lambda bi, i: (0,) * a.ndim)
    consts = [w["w_o"], w["g_ffn"], w["w_up"], w["w_conv"], w["b_conv"], w["w_down"]]
    ins = [x, o_mla, o_fox] + consts
    in_specs = [row(d), row(o_mla.shape[-1]), row(o_fox.shape[-1])] + [full(a) for a in consts]
    out_shape = [jax.ShapeDtypeStruct((b, s, d), F32)]
    out_specs = [row(d)]
    scratch = []
    if prompt:
        out_shape.append(jax.ShapeDtypeStruct((b, 8, 2 * d_ff), F32))
        out_specs.append(pl.BlockSpec((None, 8, 2 * d_ff), lambda bi, i: (bi, 0, 0)))
        scratch = [pltpu.VMEM((tm + 8, 2 * d_ff), F32)]
    else:
        ins += [state[0], state[1]]
        in_specs += [row(2 * d_ff), row(2 * d_ff)]
        out_shape.append(jax.ShapeDtypeStruct((b, s, 2 * d_ff), F32))
        out_specs.append(row(2 * d_ff))
    return pl.pallas_call(
        functools.partial(_post_kernel, tm=tm, d_ff=d_ff, bounds=_ffn_bounds(d_ff), prompt=prompt),
        grid=grid, in_specs=in_specs, out_specs=out_specs, out_shape=out_shape,
        scratch_shapes=scratch,
        compiler_params=pltpu.CompilerParams(
            dimension_semantics=("parallel", "arbitrary"), vmem_limit_bytes=VMEM_LIMIT),
        name="post_prompt" if prompt else "post_sample",
    )(*ins)


def _pad_cols(a, width, offset=0):
    return jnp.pad(a, ((0, 0), (offset, width - offset - a.shape[1])))


def _prepare_weights(g_attn, w_in, b_f, g_qa, w_qb, g_kva, w_kvb, g_qn, g_qr, g_kn, g_kr, g_fq,
                     g_fk, w_o, g_ffn, w_up, w_conv, b_conv, w_down):
    hd = N_HEADS * HEAD_DIM
    o_kva, o_kr = Q_LORA, Q_LORA + KV_LORA
    o_fq = o_kr + ROPE_DIM
    o_fk, o_fv, o_ff = o_fq + hd, o_fq + 2 * hd, o_fq + 3 * hd
    w_in_p = jnp.concatenate([
        w_in[:, :o_kr],
        _pad_cols(w_in[:, o_kr:o_fq], LANES, ROPE_OFF),
        w_in[:, o_fq:o_ff],
        _pad_cols(w_in[:, o_ff:], LANES)], axis=1).astype(BF16)
    d_q = NOPE_DIM + ROPE_DIM
    w_qb_p = jnp.pad(w_qb.reshape(Q_LORA, N_HEADS, d_q), ((0, 0), (0, 0), (0, LANES - d_q)))
    w_qb_p = w_qb_p.reshape(Q_LORA, N_HEADS * LANES).astype(BF16)
    w_kvb3 = w_kvb.reshape(KV_LORA, N_HEADS, NOPE_DIM + HEAD_DIM)
    wk3, wv3 = w_kvb3[:, :, :NOPE_DIM], w_kvb3[:, :, NOPE_DIM:]
    wk_pad = jnp.pad(wk3, ((0, 0), (0, 0), (0, LANES - NOPE_DIM))).reshape(KV_LORA, N_HEADS * LANES)
    wv = wv3.reshape(KV_LORA, hd)
    wk_abs = jnp.pad(jnp.transpose(wk3, (2, 1, 0)), ((0, LANES - NOPE_DIM), (0, 0), (0, 0)))
    half = ROPE_DIM // 2
    inv = ROPE_THETA ** (-jnp.arange(half, dtype=F32) / half)
    inv_lane = _pad_cols(jnp.tile(inv, 2)[None, :], LANES, ROPE_OFF)
    row = lambda a: a[None, :].astype(F32)
    return {
        "g_attn": row(g_attn), "w_in": w_in_p, "g_qa": row(g_qa), "w_qb": w_qb_p,
        "g_kva": row(g_kva), "w_kvb": jnp.concatenate([wk_pad, wv], axis=1).astype(BF16),
        "g_q": _pad_cols(jnp.concatenate([g_qn, g_qr])[None, :], LANES),
        "g_k": _pad_cols(row(g_kn), LANES), "g_kr": _pad_cols(row(g_kr), LANES, ROPE_OFF),
        "g_fq": row(jnp.tile(g_fq, 2)), "g_fk": row(jnp.tile(g_fk, 2)),
        "b_f": _pad_cols(row(b_f), LANES), "inv_lane": inv_lane,
        "wk_abs": wk_abs.reshape(LANES, N_HEADS * KV_LORA).astype(BF16),
        "wk_t": wk3.reshape(KV_LORA, hd).T.astype(BF16), "wv": wv.astype(BF16),
        "w_o": w_o.astype(BF16), "g_ffn": row(g_ffn), "w_up": w_up.astype(BF16),
        "w_conv": w_conv.astype(F32), "b_conv": row(b_conv), "w_down": w_down.astype(BF16),
    }


def _tile(n, pref):
    t = min(n, pref)
    assert n % t == 0, (n, pref)
    return t


def kernel(x_prompt, x_sample, cache_mla_ckv, cache_mla_krope, cache_fox_k, cache_fox_v, cache_fox_logf, state_ffn_conv, page_table, g_attn, w_in, b_f, g_qa, w_qb, g_kva, w_kvb, g_qn, g_qr, g_kn, g_kr, g_fq, g_fk, w_o, g_ffn, w_up, w_conv, b_conv, w_down):
    w = _prepare_weights(g_attn, w_in, b_f, g_qa, w_qb, g_kva, w_kvb, g_qn, g_qr, g_kn, g_kr,
                         g_fq, g_fk, w_o, g_ffn, w_up, w_conv, b_conv, w_down)
    bp, s, _ = x_prompt.shape
    bd, t_dec, d = x_sample.shape
    assert t_dec == 1
    n_pool, page = cache_mla_ckv.shape[:2]
    past = page_table.shape[1] * page
    hd = N_HEADS * HEAD_DIM

    (p_ckv, p_kr, p_fk, p_fv, p_lf, qm, km, vm, qf, kf, vf) = _project(
        x_prompt, w, prompt=True, pos_base=0, tm=_tile(s, 256))
    tq = _tile(s, ATTN_TQ)
    o_mla = _prompt_attention(qm, km, vm, tq=tq)
    o_fox = _prompt_attention(qf, kf, vf, tq=tq)
    y_p, tail = _post(x_prompt, o_mla, o_fox, w, None, prompt=True, tm=_tile(s, 256))

    xs = x_sample.reshape(1, bd, d)
    s_ckv, s_kr, s_fk, s_fv, s_lf, qs, qfs = _project(
        xs, w, prompt=False, pos_base=past, tm=_tile(bd, 128))
    s_kr32 = s_kr[0, :, ROPE_OFF:ROPE_OFF + ROPE_DIM]
    pad8 = lambda a: jnp.pad(a[:, None, :], ((0, 0), (0, 7), (0, 0)))
    n_dec_pages = _tile(page_table.shape[1], MLA_DEC_PAGES)
    n_fox_pages = _tile(page_table.shape[1], FOX_DEC_PAGES)
    o_mla_s = _decode_mla(
        page_table, jnp.transpose(qs[0], (1, 0, 2)), pad8(s_ckv[0]), pad8(s_kr32), w,
        cache_mla_ckv, jnp.transpose(cache_mla_krope, (0, 2, 1)), n_pages=n_dec_pages)
    qf3 = qfs[0].reshape(bd, N_HEADS, HEAD_DIM)
    lf_new = jnp.broadcast_to(s_lf[0, :, :N_HEADS, None], (bd, N_HEADS, page))
    o_fox_s = _decode_fox(
        page_table, qf3, jnp.transpose(qf3, (0, 2, 1)), s_fk[0].reshape(bd, N_HEADS, HEAD_DIM),
        s_fv[0].reshape(bd, N_HEADS, HEAD_DIM), lf_new,
        jnp.transpose(cache_fox_k, (0, 2, 3, 1)), jnp.transpose(cache_fox_v, (0, 2, 3, 1)),
        jnp.transpose(cache_fox_logf, (0, 2, 1)), n_pages=n_fox_pages)
    state = (state_ffn_conv[:, 0, :][None], state_ffn_conv[:, 1, :][None])
    y_s, u_s = _post(xs, o_mla_s.reshape(1, bd, hd), o_fox_s.reshape(1, bd, hd), w, state,
                     prompt=False, tm=_tile(bd, 128))

    return (
        y_p, y_s.reshape(bd, 1, d),
        p_ckv, p_kr[:, :, ROPE_OFF:ROPE_OFF + ROPE_DIM],
        p_fk.reshape(bp, s, N_HEADS, HEAD_DIM), p_fv.reshape(bp, s, N_HEADS, HEAD_DIM),
        p_lf[:, :, :N_HEADS], tail[:, 6:8, :],
        s_ckv.reshape(bd, 1, KV_LORA), s_kr32.reshape(bd, 1, ROPE_DIM),
        s_fk.reshape(bd, 1, N_HEADS, HEAD_DIM), s_fv.reshape(bd, 1, N_HEADS, HEAD_DIM),
        s_lf[0, :, :N_HEADS].reshape(bd, 1, N_HEADS),
        jnp.stack([state_ffn_conv[:, 1, :], u_s[0]], axis=1),
    )
```
